```python
import jax, jax.numpy as jnp
from jax import lax
import numpy as np

D_MODEL = 1024
BATCH = 16
SEQ = 2048
DEPTH = 2

MEM_LEN = 256
HEAD_DIM = 64
N_SELF_HEADS = 12
N_KV_HEADS_SWA = 3
N_CROSS_HEADS = 4
SWA_WINDOW = 128
BLOCK = 128
ROPE_THETA = 10000.0
D_FF = 3584
N_EXPERTS = 8
TOP_K = 2
RMS_EPS = 1e-6
N_MIXERS = 2
MAX_START = 4096

SELF_W = N_SELF_HEADS * HEAD_DIM
KV_W_SWA = N_KV_HEADS_SWA * HEAD_DIM
CROSS_W = N_CROSS_HEADS * HEAD_DIM
IN_W_SWA = SELF_W + 2 * KV_W_SWA + CROSS_W
IN_W_SB = 3 * SELF_W + CROSS_W
MIX_W = SELF_W + CROSS_W

kernel_name = "hybrid_swa_stickbreak_memx_moe"


def rms_norm(x, g):
    xf = x.astype(jnp.float32)
    var = jnp.mean(xf * xf, axis=-1, keepdims=True)
    return (xf * lax.rsqrt(var + RMS_EPS) * g.astype(jnp.float32)).astype(x.dtype)


def rope(x, positions):
    inv_freq = ROPE_THETA ** (-jnp.arange(0, HEAD_DIM, 2, dtype=jnp.float32) / HEAD_DIM)
    ang = positions.astype(jnp.float32)[..., None] * inv_freq
    cos = jnp.cos(ang)[:, :, None, :]
    sin = jnp.sin(ang)[:, :, None, :]
    xf = x.astype(jnp.float32)
    x1, x2 = jnp.split(xf, 2, axis=-1)
    out = jnp.concatenate([x1 * cos - x2 * sin, x2 * cos + x1 * sin], axis=-1)
    return out.astype(x.dtype)


def swa_sink_attention(q, k, v, sinks):
    B, S, Hq, hd = q.shape
    Hkv = k.shape[2]
    G = Hq // Hkv
    nb = S // BLOCK
    scale = hd ** -0.5
    qb = q.reshape(B, nb, BLOCK, Hkv, G, hd)

    def band(t):
        tb = t.reshape(B, nb, BLOCK, Hkv, hd)
        prev = jnp.pad(tb[:, :-1], ((0, 0), (1, 0), (0, 0), (0, 0), (0, 0)))
        return jnp.concatenate([prev, tb], axis=2)

    kband, vband = band(k), band(v)
    scores = jnp.einsum('bnqhgd,bnkhd->bnhgqk', qb, kband,
                        preferred_element_type=jnp.float32) * scale
    qi = jnp.arange(BLOCK)[:, None]
    kj = jnp.arange(2 * BLOCK)[None, :]
    diff = qi + BLOCK - kj
    in_win = (diff >= 0) & (diff < SWA_WINDOW)
    blk = jnp.arange(nb)[:, None]
    key_ok = (blk * BLOCK + kj - BLOCK) >= 0
    mask = in_win[None, :, :] & key_ok[:, None, :]
    scores = jnp.where(mask[None, :, None, None], scores, -jnp.inf)
    sink = sinks.astype(jnp.float32).reshape(Hkv, G)[None, None, :, :, None, None]
    m = jnp.maximum(jnp.max(scores, axis=-1, keepdims=True), sink)
    p = jnp.exp(scores - m)
    denom = jnp.sum(p, axis=-1, keepdims=True) + jnp.exp(sink - m)
    probs = (p / denom).astype(v.dtype)
    out = jnp.einsum('bnhgqk,bnkhd->bnqhgd', probs, vband)
    return out.reshape(B, S, Hq * hd)


def stick_breaking_attention(q, k, v):
    B, S, H, hd = q.shape
    nb = S // BLOCK
    scale = hd ** -0.5
    qb = q.reshape(B, nb, BLOCK, H, hd).transpose(1, 0, 3, 2, 4)
    kt = k.transpose(0, 2, 1, 3)
    vt = v.transpose(0, 2, 1, 3)
    s_idx = jnp.arange(S)

    def one_block(args):
        q_blk, blk = args
        z = jnp.einsum('bhqd,bhkd->bhqk', q_blk, kt,
                       preferred_element_type=jnp.float32) * scale
        t_idx = blk * BLOCK + jnp.arange(BLOCK)
        causal = s_idx[None, :] < t_idx[:, None]
        log_1m = jnp.where(causal, jax.nn.log_sigmoid(-z), 0.0)
        later = lax.cumsum(log_1m, axis=3, reverse=True) - log_1m
        w = jnp.where(causal, jnp.exp(jax.nn.log_sigmoid(z) + later), 0.0)
        return jnp.einsum('bhqk,bhkd->bhqd', w.astype(vt.dtype), vt)

    out = lax.map(one_block, (qb, jnp.arange(nb)))
    return out.transpose(1, 0, 3, 2, 4).reshape(B, S, H * hd)


def memory_cross_attention(qc, mem_n, w_mem_kv):
    B, S, _ = qc.shape
    kv = mem_n @ w_mem_kv
    km, vm = jnp.split(kv, 2, axis=-1)
    qh = qc.reshape(B, S, N_CROSS_HEADS, HEAD_DIM)
    km = km.reshape(B, MEM_LEN, N_CROSS_HEADS, HEAD_DIM)
    vm = vm.reshape(B, MEM_LEN, N_CROSS_HEADS, HEAD_DIM)
    scores = jnp.einsum('bshd,bmhd->bhsm', qh, km,
                        preferred_element_type=jnp.float32) * (HEAD_DIM ** -0.5)
    probs = jax.nn.softmax(scores, axis=-1).astype(vm.dtype)
    out = jnp.einsum('bhsm,bmhd->bshd', probs, vm)
    return out.reshape(B, S, CROSS_W)


def token_mixing(h, mem, positions, mixer, p):
    B, S, _ = h.shape
    xn = rms_norm(h, p['attn_pre_g'])
    proj = xn @ p['w_in']
    if mixer == 0:
        q, k, v, qc = jnp.split(proj, [SELF_W, SELF_W + KV_W_SWA, SELF_W + 2 * KV_W_SWA], axis=-1)
        q = rope(q.reshape(B, S, N_SELF_HEADS, HEAD_DIM), positions)
        k = rope(k.reshape(B, S, N_KV_HEADS_SWA, HEAD_DIM), positions)
        v = v.reshape(B, S, N_KV_HEADS_SWA, HEAD_DIM)
        self_out = swa_sink_attention(q, k, v, p['sinks'])
    else:
        q, k, v, qc = jnp.split(proj, [SELF_W, 2 * SELF_W, 3 * SELF_W], axis=-1)
        q = q.reshape(B, S, N_SELF_HEADS, HEAD_DIM)
        k = k.reshape(B, S, N_SELF_HEADS, HEAD_DIM)
        v = v.reshape(B, S, N_SELF_HEADS, HEAD_DIM)
        self_out = stick_breaking_attention(q, k, v)
    mem_n = rms_norm(mem, p['mem_g'])
    cross_out = memory_cross_attention(qc, mem_n, p['w_mem_kv'])
    mixed = jnp.concatenate([self_out, cross_out], axis=-1) @ p['w_out']
    return h + rms_norm(mixed, p['attn_post_g'])


def swiglu(x, w_gate, w_up, w_down):
    return (jax.nn.silu(x @ w_gate) * (x @ w_up)) @ w_down


def moe_swiglu(x, w_router, w_gate, w_up, w_down):
    B, S, D = x.shape
    xt = x.reshape(B * S, D)
    logits = jnp.matmul(xt, w_router, preferred_element_type=jnp.float32)
    top_vals, top_idx = lax.top_k(logits, TOP_K)
    top_w = jax.nn.softmax(top_vals, axis=-1)
    gates = jnp.sum(jax.nn.one_hot(top_idx, N_EXPERTS, dtype=jnp.float32) * top_w[..., None], axis=1)
    out = jnp.zeros((B * S, D), jnp.float32)
    for e in range(N_EXPERTS):
        y = swiglu(xt, w_gate[e], w_up[e], w_down[e]).astype(jnp.float32)
        out = out + gates[:, e:e + 1] * y
    return out.astype(x.dtype).reshape(B, S, D)


def channel_mixing(h, layer_idx, p):
    xn = rms_norm(h, p['ffn_pre_g'])
    if layer_idx % 2 == 0:
        y = swiglu(xn, p['w_gate'], p['w_up'], p['w_down'])
    else:
        y = moe_swiglu(xn, p['w_router'], p['w_gate'], p['w_up'], p['w_down'])
    return h + rms_norm(y, p['ffn_post_g'])


def setup_inputs(seed: int = 0) -> dict:
    key = jax.random.key(seed)
    ks = jax.random.split(key, 32)
    f32 = jnp.float32

    def nrm(k, shape, fan_in):
        return jax.random.normal(k, shape, f32) * (fan_in ** -0.5)

    def gain(k):
        return 1.0 + 0.05 * jax.random.normal(k, (D_MODEL,), f32)

    x = jax.random.normal(ks[0], (BATCH, SEQ, D_MODEL), f32)
    mem = jax.random.normal(ks[1], (BATCH, MEM_LEN, D_MODEL), f32)
    start = jax.random.randint(ks[2], (BATCH, 1), 0, MAX_START, dtype=jnp.int32)
    positions = start + jnp.arange(SEQ, dtype=jnp.int32)[None, :]
    return {
        "x": x,
        "mem": mem,
        "positions": positions,
        "l0_attn_pre_g": gain(ks[3]),
        "l0_mem_g": gain(ks[4]),
        "l0_w_in": nrm(ks[5], (D_MODEL, IN_W_SWA), D_MODEL),
        "l0_w_mem_kv": nrm(ks[6], (D_MODEL, 2 * CROSS_W), D_MODEL),
        "l0_sinks": 0.5 * jax.random.normal(ks[7], (N_SELF_HEADS,), f32),
        "l0_w_out": nrm(ks[8], (MIX_W, D_MODEL), MIX_W),
        "l0_attn_post_g": gain(ks[9]),
        "l0_ffn_pre_g": gain(ks[10]),
        "l0_w_gate": nrm(ks[11], (D_MODEL, D_FF), D_MODEL),
        "l0_w_up": nrm(ks[12], (D_MODEL, D_FF), D_MODEL),
        "l0_w_down": nrm(ks[13], (D_FF, D_MODEL), D_FF),
        "l0_ffn_post_g": gain(ks[14]),
        "l1_attn_pre_g": gain(ks[15]),
        "l1_mem_g": gain(ks[16]),
        "l1_w_in": nrm(ks[17], (D_MODEL, IN_W_SB), D_MODEL),
        "l1_w_mem_kv": nrm(ks[18], (D_MODEL, 2 * CROSS_W), D_MODEL),
        "l1_w_out": nrm(ks[19], (MIX_W, D_MODEL), MIX_W),
        "l1_attn_post_g": gain(ks[20]),
        "l1_ffn_pre_g": gain(ks[21]),
        "l1_w_router": nrm(ks[22], (D_MODEL, N_EXPERTS), D_MODEL),
        "l1_w_gate": nrm(ks[23], (N_EXPERTS, D_MODEL, D_FF), D_MODEL),
        "l1_w_up": nrm(ks[24], (N_EXPERTS, D_MODEL, D_FF), D_MODEL),
        "l1_w_down": nrm(ks[25], (N_EXPERTS, D_FF, D_MODEL), D_FF),
        "l1_ffn_post_g": gain(ks[26]),
    }


def reference(x, mem, positions,
              l0_attn_pre_g, l0_mem_g, l0_w_in, l0_w_mem_kv, l0_sinks, l0_w_out, l0_attn_post_g,
              l0_ffn_pre_g, l0_w_gate, l0_w_up, l0_w_down, l0_ffn_post_g,
              l1_attn_pre_g, l1_mem_g, l1_w_in, l1_w_mem_kv, l1_w_out, l1_attn_post_g,
              l1_ffn_pre_g, l1_w_router, l1_w_gate, l1_w_up, l1_w_down, l1_ffn_post_g):
    layers = [
        dict(attn_pre_g=l0_attn_pre_g, mem_g=l0_mem_g, w_in=l0_w_in, w_mem_kv=l0_w_mem_kv,
             sinks=l0_sinks, w_out=l0_w_out, attn_post_g=l0_attn_post_g,
             ffn_pre_g=l0_ffn_pre_g, w_gate=l0_w_gate, w_up=l0_w_up, w_down=l0_w_down,
             ffn_post_g=l0_ffn_post_g),
        dict(attn_pre_g=l1_attn_pre_g, mem_g=l1_mem_g, w_in=l1_w_in, w_mem_kv=l1_w_mem_kv,
             w_out=l1_w_out, attn_post_g=l1_attn_post_g,
             ffn_pre_g=l1_ffn_pre_g, w_router=l1_w_router, w_gate=l1_w_gate, w_up=l1_w_up,
             w_down=l1_w_down, ffn_post_g=l1_ffn_post_g),
    ]
    h = x
    for i in range(DEPTH):
        p = layers[i]
        h = token_mixing(h, mem, positions, i % N_MIXERS, p)
        h = channel_mixing(h, i, p)
    return h
```

```python
import functools

import jax
import jax.numpy as jnp
from jax import lax
from jax.experimental import pallas as pl
from jax.experimental.pallas import tpu as pltpu

D_MODEL = 1024
HEAD_DIM = 64
N_SELF_HEADS = 12
N_KV_HEADS_SWA = 3
N_CROSS_HEADS = 4
SWA_WINDOW = 128
BLOCK = 128
ROPE_THETA = 10000.0
D_FF = 3584
N_EXPERTS = 8
RMS_EPS = 1e-6

SELF_W = N_SELF_HEADS * HEAD_DIM
CROSS_W = N_CROSS_HEADS * HEAD_DIM
KV_DUP_W = N_KV_HEADS_SWA * 2 * HEAD_DIM
LANES = 128
Q_SCALE = HEAD_DIM ** -0.5
NEG_BIG = -1e30

VMEM_LIMIT = 56 * 1024 * 1024

F32 = jnp.float32
BF16 = jnp.bfloat16


def _rms(xf, g):
    var = jnp.mean(xf * xf, axis=-1, keepdims=True)
    return xf * lax.rsqrt(var + RMS_EPS) * g


def _lane_iota(shape):
    return lax.broadcasted_iota(jnp.int32, shape, len(shape) - 1)


def _dot(a, b):
    return jnp.dot(a, b, preferred_element_type=F32)


def _dot_nt(a, b):
    return lax.dot_general(a, b, (((1,), (1,)), ((), ())), preferred_element_type=F32)


def _half_masked_pair(x2):
    lo = _lane_iota(x2.shape) < HEAD_DIM
    zero = jnp.zeros_like(x2)
    return jnp.concatenate([jnp.where(lo, x2, zero), jnp.where(lo, zero, x2)], axis=0)


def _merge_halves(o, r):
    lo = _lane_iota((r, LANES)) < HEAD_DIM
    return jnp.where(lo, o[:r], o[r:])


def _mem_kv_kernel(mem_ref, g_ref, w_ref, k_ref, v_ref):
    xn = _rms(mem_ref[...], g_ref[...]).astype(BF16)
    kv = _dot(xn, w_ref[...])
    k_ref[...] = kv[:, :CROSS_W].astype(BF16)
    v_ref[...] = kv[:, CROSS_W:].astype(BF16)


def _mem_kv(mem2d, g, w_bf16, tm=512):
    n = mem2d.shape[0]
    return pl.pallas_call(
        _mem_kv_kernel,
        grid=(n // tm,),
        in_specs=[
            pl.BlockSpec((tm, D_MODEL), lambda i: (i, 0)),
            pl.BlockSpec((1, D_MODEL), lambda i: (0, 0)),
            pl.BlockSpec((D_MODEL, 2 * CROSS_W), lambda i: (0, 0)),
        ],
        out_specs=[
            pl.BlockSpec((tm, CROSS_W), lambda i: (i, 0)),
            pl.BlockSpec((tm, CROSS_W), lambda i: (i, 0)),
        ],
        out_shape=[jax.ShapeDtypeStruct((n, CROSS_W), BF16)] * 2,
        compiler_params=pltpu.CompilerParams(
            dimension_semantics=("arbitrary",), vmem_limit_bytes=VMEM_LIMIT),
        name="mem_kv",
    )(mem2d, g, w_bf16)


def _rope_chunk(xc, cos, sin, first_half):
    rot = jnp.where(first_half, -pltpu.roll(xc, LANES - HEAD_DIM // 2, axis=1),
                    pltpu.roll(xc, HEAD_DIM // 2, axis=1))
    return xc * cos + rot * sin


def _proj_swa_kernel(h_ref, g_ref, w_ref, pos_ref, invf_ref, q_ref, k_ref, v_ref, qc_ref):
    xn = _rms(h_ref[...], g_ref[...]).astype(BF16)
    proj = _dot(xn, w_ref[...])
    ang = pos_ref[...].astype(F32) * invf_ref[...]
    cos = jnp.cos(ang)
    sin = jnp.sin(ang)
    first_half = (_lane_iota(cos.shape) % HEAD_DIM) < (HEAD_DIM // 2)
    for c in range(SELF_W // LANES):
        xc = proj[:, c * LANES:(c + 1) * LANES]
        q_ref[:, c * LANES:(c + 1) * LANES] = (
            _rope_chunk(xc, cos, sin, first_half) * Q_SCALE).astype(BF16)
    for c in range(KV_DUP_W // LANES):
        xc = proj[:, SELF_W + c * LANES:SELF_W + (c + 1) * LANES]
        k_ref[:, c * LANES:(c + 1) * LANES] = _rope_chunk(xc, cos, sin, first_half).astype(BF16)
    v0 = SELF_W + KV_DUP_W
    v_ref[...] = proj[:, v0:v0 + KV_DUP_W].astype(BF16)
    qc_ref[...] = (proj[:, v0 + KV_DUP_W:] * Q_SCALE).astype(BF16)


def _proj_swa(h, g, w_bf16, pos, invf, tm=512):
    n = h.shape[0]
    w_cols = w_bf16.shape[1]
    row = lambda i: (i, 0)
    fixed = lambda i: (0, 0)
    return pl.pallas_call(
        _proj_swa_kernel,
        grid=(n // tm,),
        in_specs=[
            pl.BlockSpec((tm, D_MODEL), row),
            pl.BlockSpec((1, D_MODEL), fixed),
            pl.BlockSpec((D_MODEL, w_cols), fixed),
            pl.BlockSpec((tm, 1), row),
            pl.BlockSpec((1, LANES), fixed),
        ],
        out_specs=[
            pl.BlockSpec((tm, SELF_W), row),
            pl.BlockSpec((tm, KV_DUP_W), row),
            pl.BlockSpec((tm, KV_DUP_W), row),
            pl.BlockSpec((tm, CROSS_W), row),
        ],
        out_shape=[
            jax.ShapeDtypeStruct((n, SELF_W), BF16),
            jax.ShapeDtypeStruct((n, KV_DUP_W), BF16),
            jax.ShapeDtypeStruct((n, KV_DUP_W), BF16),
            jax.ShapeDtypeStruct((n, CROSS_W), BF16),
        ],
        compiler_params=pltpu.CompilerParams(
            dimension_semantics=("arbitrary",), vmem_limit_bytes=VMEM_LIMIT),
        name="proj_swa",
    )(h, g, w_bf16, pos, invf)


def _proj_sb_kernel(h_ref, g_ref, w_ref, q_ref, k_ref, v_ref, qc_ref):
    xn = _rms(h_ref[...], g_ref[...]).astype(BF16)
    proj = _dot(xn, w_ref[...])
    q_ref[...] = (proj[:, :SELF_W] * Q_SCALE).astype(BF16)
    k_ref[...] = proj[:, SELF_W:2 * SELF_W].astype(BF16)
    v_ref[...] = proj[:, 2 * SELF_W:3 * SELF_W].astype(BF16)
    qc_ref[...] = (proj[:, 3 * SELF_W:] * Q_SCALE).astype(BF16)


def _proj_sb(h, g, w_bf16, tm=512):
    n = h.shape[0]
    row = lambda i: (i, 0)
    fixed = lambda i: (0, 0)
    return pl.pallas_call(
        _proj_sb_kernel,
        grid=(n // tm,),
        in_specs=[
            pl.BlockSpec((tm, D_MODEL), row),
            pl.BlockSpec((1, D_MODEL), fixed),
            pl.BlockSpec((D_MODEL, w_bf16.shape[1]), fixed),
        ],
        out_specs=[
            pl.BlockSpec((tm, SELF_W), row),
            pl.BlockSpec((tm, SELF_W), row),
            pl.BlockSpec((tm, SELF_W), row),
            pl.BlockSpec((tm, CROSS_W), row),
        ],
        out_shape=[
            jax.ShapeDtypeStruct((n, SELF_W), BF16),
            jax.ShapeDtypeStruct((n, SELF_W), BF16),
            jax.ShapeDtypeStruct((n, SELF_W), BF16),
            jax.ShapeDtypeStruct((n, CROSS_W), BF16),
        ],
        compiler_params=pltpu.CompilerParams(
            dimension_semantics=("arbitrary",), vmem_limit_bytes=VMEM_LIMIT),
        name="proj_sb",
    )(h, g, w_bf16)


def _cross_attention(qc_ref, km_ref, vm_ref, mixed_ref, tq):
    for p in range(CROSS_W // LANES):
        sl = slice(p * LANES, (p + 1) * LANES)
        qs = _half_masked_pair(qc_ref[:, sl])
        s = _dot_nt(qs, km_ref[:, sl])
        m = jnp.max(s, axis=-1, keepdims=True)
        e = jnp.exp(s - m)
        probs = (e / jnp.sum(e, axis=-1, keepdims=True)).astype(BF16)
        o = _dot(probs, vm_ref[:, sl])
        mixed_ref[:, SELF_W + p * LANES:SELF_W + (p + 1) * LANES] = _merge_halves(o, tq).astype(BF16)


def _out_proj_residual(mixed_ref, wo_ref, g_ref, h_ref):
    mixed = _dot(mixed_ref[...], wo_ref[...])
    return h_ref[...] + _rms(mixed, g_ref[...])


def _swa_kernel(sinks_ref, q_ref, k_ref, v_ref, qc_ref, km_ref, vm_ref, wo_ref, g_ref, h_ref,
                o_ref, mixed_ref, *, tq):
    i = pl.program_id(1)
    blocks = tq // BLOCK
    group_rows = 4 * BLOCK

    qi = lax.broadcasted_iota(jnp.int32, (group_rows, 2 * BLOCK), 0) % BLOCK
    kj = lax.broadcasted_iota(jnp.int32, (group_rows, 2 * BLOCK), 1)
    diff = qi + BLOCK - kj
    in_win = (diff >= 0) & (diff < SWA_WINDOW)
    row_head = lax.broadcasted_iota(jnp.int32, (group_rows, 1), 0) // BLOCK

    for blk in range(blocks):
        n = i * blocks + blk
        r0 = blk * BLOCK
        cur0 = pl.multiple_of(n * BLOCK, BLOCK)
        prev0 = pl.multiple_of(jnp.maximum(n - 1, 0) * BLOCK, BLOCK)
        mask = in_win & ((kj >= BLOCK) | (n > 0))
        for g in range(N_KV_HEADS_SWA):
            gs = slice(g * LANES, (g + 1) * LANES)
            kcat = jnp.concatenate([k_ref[pl.ds(prev0, BLOCK), gs], k_ref[pl.ds(cur0, BLOCK), gs]], axis=0)
            vcat = jnp.concatenate([v_ref[pl.ds(prev0, BLOCK), gs], v_ref[pl.ds(cur0, BLOCK), gs]], axis=0)
            c0 = g * 2 * LANES
            qs = jnp.concatenate(
                [_half_masked_pair(q_ref[r0:r0 + BLOCK, c0:c0 + LANES]),
                 _half_masked_pair(q_ref[r0:r0 + BLOCK, c0 + LANES:c0 + 2 * LANES])], axis=0)
            s = jnp.where(mask, _dot_nt(qs, kcat), NEG_BIG)
            sink = jnp.zeros((group_rows, 1), F32)
            for hh in range(4):
                sink = jnp.where(row_head == hh, sinks_ref[4 * g + hh], sink)
            m = jnp.maximum(jnp.max(s, axis=-1, keepdims=True), sink)
            e = jnp.exp(s - m)
            denom = jnp.sum(e, axis=-1, keepdims=True) + jnp.exp(sink - m)
            probs = (e / denom).astype(BF16)
            o = _dot(probs, vcat)
            mixed_ref[r0:r0 + BLOCK, c0:c0 + LANES] = _merge_halves(o[:2 * BLOCK], BLOCK).astype(BF16)
            mixed_ref[r0:r0 + BLOCK, c0 + LANES:c0 + 2 * LANES] = _merge_halves(o[2 * BLOCK:], BLOCK).astype(BF16)

    _cross_attention(qc_ref, km_ref, vm_ref, mixed_ref, tq)
    o_ref[...] = _out_proj_residual(mixed_ref, wo_ref, g_ref, h_ref)


def _swa_layer(sinks, q, k, v, qc, km, vm, wo_bf16, g, h, batch, seq, tq=512):
    n = h.shape[0]
    mem_len = km.shape[0] // batch
    tiles = seq // tq
    row = lambda b, i, s: (b * tiles + i, 0)
    per_b = lambda b, i, s: (b, 0)
    fixed = lambda b, i, s: (0, 0)
    grid_spec = pltpu.PrefetchScalarGridSpec(
        num_scalar_prefetch=1,
        grid=(batch, tiles),
        in_specs=[
            pl.BlockSpec((tq, SELF_W), row),
            pl.BlockSpec((seq, KV_DUP_W), per_b),
            pl.BlockSpec((seq, KV_DUP_W), per_b),
            pl.BlockSpec((tq, CROSS_W), row),
            pl.BlockSpec((mem_len, CROSS_W), per_b),
            pl.BlockSpec((mem_len, CROSS_W), per_b),
            pl.BlockSpec((D_MODEL, D_MODEL), fixed),
            pl.BlockSpec((1, D_MODEL), fixed),
            pl.BlockSpec((tq, D_MODEL), row),
        ],
        out_specs=pl.BlockSpec((tq, D_MODEL), row),
        scratch_shapes=[pltpu.VMEM((tq, D_MODEL), BF16)],
    )
    return pl.pallas_call(
        functools.partial(_swa_kernel, tq=tq),
        grid_spec=grid_spec,
        out_shape=jax.ShapeDtypeStruct((n, D_MODEL), F32),
        compiler_params=pltpu.CompilerParams(
            dimension_semantics=("arbitrary", "arbitrary"), vmem_limit_bytes=VMEM_LIMIT),
        name="swa_layer",
    )(sinks, q, k, v, qc, km, vm, wo_bf16, g, h)


def _softplus(z):
    return jnp.maximum(z, 0.0) + jnp.log(1.0 + jnp.exp(-jnp.abs(z)))


def _sb_kernel(q_ref, k_ref, v_ref, qc_ref, km_ref, vm_ref, wo_ref, g_ref, h_ref, g2_ref, wr_ref,
               o_ref, xn_ref, route_ref, mixed_ref, *, tq):
    i = pl.program_id(1)
    tk = tq
    rows = 2 * tq

    r_idx = lax.broadcasted_iota(jnp.int32, (rows, tk), 0) % tq
    c_idx = lax.broadcasted_iota(jnp.int32, (rows, tk), 1)
    causal = c_idx < r_idx
    kr = lax.broadcasted_iota(jnp.int32, (tk, tk), 0)
    kc = lax.broadcasted_iota(jnp.int32, (tk, tk), 1)
    suffix = jnp.where(kr >= kc, 1.0, 0.0).astype(BF16)

    for p in range(SELF_W // LANES):
        sl = slice(p * LANES, (p + 1) * LANES)
        qs = _half_masked_pair(q_ref[:, sl])

        def block(j, carry, acc, diagonal):
            k0 = pl.multiple_of(j * tk, tk)
            z = _dot_nt(qs, k_ref[pl.ds(k0, tk), sl])
            sp = _softplus(z)
            if diagonal:
                sp = jnp.where(causal, sp, 0.0)
            s_incl = _dot(sp.astype(BF16), suffix)
            w = jnp.exp(z - s_incl - carry)
            if diagonal:
                w = jnp.where(causal, w, 0.0)
            acc = acc + _dot(w.astype(BF16), v_ref[pl.ds(k0, tk), sl])
            return carry + s_incl[:, 0:1], acc

        carry, acc = block(i, jnp.zeros((rows, 1), F32), jnp.zeros((rows, LANES), F32), True)

        def body(t, state):
            return block(i - 1 - t, state[0], state[1], False)

        carry, acc = lax.fori_loop(0, i, body, (carry, acc))
        mixed_ref[:, sl] = _merge_halves(acc, tq).astype(BF16)

    _cross_attention(qc_ref, km_ref, vm_ref, mixed_ref, tq)
    h1 = _out_proj_residual(mixed_ref, wo_ref, g_ref, h_ref)
    o_ref[...] = h1

    xn = _rms(h1, g2_ref[...])
    xn_ref[...] = xn.astype(BF16)
    logits = jnp.dot(xn, wr_ref[...], preferred_element_type=F32, precision=lax.Precision.HIGHEST)
    lane = _lane_iota(logits.shape)
    l1 = jnp.where(lane < N_EXPERTS, logits, NEG_BIG)
    m1 = jnp.max(l1, axis=-1, keepdims=True)
    i1 = jnp.min(jnp.where(l1 == m1, lane, LANES), axis=-1, keepdims=True)
    l2 = jnp.where(lane == i1, NEG_BIG, l1)
    m2 = jnp.max(l2, axis=-1, keepdims=True)
    i2 = jnp.min(jnp.where(l2 == m2, lane, LANES), axis=-1, keepdims=True)
    e2 = jnp.exp(m2 - m1)
    w1 = 1.0 / (1.0 + e2)
    w2 = e2 / (1.0 + e2)
    lane8 = _lane_iota((tq, 8))
    route_ref[...] = jnp.where(lane8 == 0, i1.astype(F32),
                     jnp.where(lane8 == 1, i2.astype(F32),
                     jnp.where(lane8 == 2, w1, jnp.where(lane8 == 3, w2, 0.0))))


def _sb_layer(q, k, v, qc, km, vm, wo_bf16, g, h, g2, wr_pad, batch, seq, tq=256):
    n = h.shape[0]
    mem_len = km.shape[0] // batch
    tiles = seq // tq
    row = lambda b, i: (b * tiles + i, 0)
    per_b = lambda b, i: (b, 0)
    fixed = lambda b, i: (0, 0)
    return pl.pallas_call(
        functools.partial(_sb_kernel, tq=tq),
        grid=(batch, tiles),
        in_specs=[
            pl.BlockSpec((tq, SELF_W), row),
            pl.BlockSpec((seq, SELF_W), per_b),
            pl.BlockSpec((seq, SELF_W), per_b),
            pl.BlockSpec((tq, CROSS_W), row),
            pl.BlockSpec((mem_len, CROSS_W), per_b),
            pl.BlockSpec((mem_len, CROSS_W), per_b),
            pl.BlockSpec((D_MODEL, D_MODEL), fixed),
            pl.BlockSpec((1, D_MODEL), fixed),
            pl.BlockSpec((tq, D_MODEL), row),
            pl.BlockSpec((1, D_MODEL), fixed),
            pl.BlockSpec((D_MODEL, LANES), fixed),
        ],
        out_specs=[
            pl.BlockSpec((tq, D_MODEL), row),
            pl.BlockSpec((tq, D_MODEL), row),
            pl.BlockSpec((tq, 8), row),
        ],
        out_shape=[
            jax.ShapeDtypeStruct((n, D_MODEL), F32),
            jax.ShapeDtypeStruct((n, D_MODEL), BF16),
            jax.ShapeDtypeStruct((n, 8), F32),
        ],
        scratch_shapes=[pltpu.VMEM((tq, D_MODEL), BF16)],
        compiler_params=pltpu.CompilerParams(
            dimension_semantics=("arbitrary", "arbitrary"), vmem_limit_bytes=VMEM_LIMIT),
        name="sb_layer",
    )(q, k, v, qc, km, vm, wo_bf16, g, h, g2, wr_pad)


def _dense_ffn_kernel(h_ref, g1_ref, wg_ref, wu_ref, wd_ref, g2_ref, o_ref, xn_ref, acc_ref):
    f = pl.program_id(1)

    @pl.when(f == 0)
    def _():
        xn_ref[...] = _rms(h_ref[...], g1_ref[...]).astype(BF16)

    xn = xn_ref[...]
    gate = _dot(xn, wg_ref[...])
    up = _dot(xn, wu_ref[...])
    act = (gate * jax.nn.sigmoid(gate) * up).astype(BF16)
    part = _dot(act, wd_ref[...])

    @pl.when(f == 0)
    def _():
        acc_ref[...] = part

    @pl.when(f > 0)
    def _():
        acc_ref[...] += part

    @pl.when(f == pl.num_programs(1) - 1)
    def _():
        o_ref[...] = h_ref[...] + _rms(acc_ref[...], g2_ref[...])


def _dense_ffn(h, g1, wg, wu, wd, g2, tm=1024, tf=512):
    n = h.shape[0]
    row = lambda i, f: (i, 0)
    fixed = lambda i, f: (0, 0)
    return pl.pallas_call(
        _dense_ffn_kernel,
        grid=(n // tm, D_FF // tf),
        in_specs=[
            pl.BlockSpec((tm, D_MODEL), row),
            pl.BlockSpec((1, D_MODEL), fixed),
            pl.BlockSpec((D_MODEL, tf), lambda i, f: (0, f)),
            pl.BlockSpec((D_MODEL, tf), lambda i, f: (0, f)),
            pl.BlockSpec((tf, D_MODEL), lambda i, f: (f, 0)),
            pl.BlockSpec((1, D_MODEL), fixed),
        ],
        out_specs=pl.BlockSpec((tm, D_MODEL), row),
        out_shape=jax.ShapeDtypeStruct((n, D_MODEL), F32),
        scratch_shapes=[pltpu.VMEM((tm, D_MODEL), BF16), pltpu.VMEM((tm, D_MODEL), F32)],
        compiler_params=pltpu.CompilerParams(
            dimension_semantics=("arbitrary", "arbitrary"), vmem_limit_bytes=VMEM_LIMIT),
        name="dense_ffn",
    )(h, g1, wg, wu, wd, g2)


def _moe_ffn_kernel(te_ref, used_ref, xs_ref, wg_ref, wu_ref, wd_ref, o_ref, acc_ref):
    i = pl.program_id(0)
    f = pl.program_id(1)

    @pl.when(i < used_ref[0])
    def _():
        xs = xs_ref[...]
        gate = _dot(xs, wg_ref[...])
        up = _dot(xs, wu_ref[...])
        act = (gate * jax.nn.sigmoid(gate) * up).astype(BF16)
        part = _dot(act, wd_ref[...])

        @pl.when(f == 0)
        def _():
            acc_ref[...] = part

        @pl.when(f > 0)
        def _():
            acc_ref[...] += part

        @pl.when(f == pl.num_programs(1) - 1)
        def _():
            o_ref[...] = acc_ref[...]

    @pl.when((i >= used_ref[0]) & (f == pl.num_programs(1) - 1))
    def _():
        o_ref[...] = jnp.zeros_like(o_ref)


def _moe_ffn(tile_expert, used, xs, wg, wu, wd, tm, tf=512):
    p_rows = xs.shape[0]
    nf = D_FF // tf

    def f_eff(i, f, used_ref):
        return jnp.where(i < used_ref[0], f, nf - 1)

    grid_spec = pltpu.PrefetchScalarGridSpec(
        num_scalar_prefetch=2,
        grid=(p_rows // tm, nf),
        in_specs=[
            pl.BlockSpec((tm, D_MODEL), lambda i, f, te, u: (i, 0)),
            pl.BlockSpec((None, D_MODEL, tf), lambda i, f, te, u: (te[i], 0, f_eff(i, f, u))),
            pl.BlockSpec((None, D_MODEL, tf), lambda i, f, te, u: (te[i], 0, f_eff(i, f, u))),
            pl.BlockSpec((None, tf, D_MODEL), lambda i, f, te, u: (te[i], f_eff(i, f, u), 0)),
        ],
        out_specs=pl.BlockSpec((tm, D_MODEL), lambda i, f, te, u: (i, 0)),
        scratch_shapes=[pltpu.VMEM((tm, D_MODEL), F32)],
    )
    return pl.pallas_call(
        _moe_ffn_kernel,
        grid_spec=grid_spec,
        out_shape=jax.ShapeDtypeStruct((p_rows, D_MODEL), F32),
        compiler_params=pltpu.CompilerParams(
            dimension_semantics=("arbitrary", "arbitrary"), vmem_limit_bytes=VMEM_LIMIT),
        name="moe_ffn",
    )(tile_expert, used, xs, wg, wu, wd)


def _combine_kernel(h_ref, ya_ref, yb_ref, route_ref, g_ref, o_ref):
    r = route_ref[...]
    y = r[:, 2:3] * ya_ref[...] + r[:, 3:4] * yb_ref[...]
    o_ref[...] = h_ref[...] + _rms(y, g_ref[...])


def _combine(h, ya, yb, route, g, tm=512):
    n = h.shape[0]
    row = lambda i: (i, 0)
    return pl.pallas_call(
        _combine_kernel,
        grid=(n // tm,),
        in_specs=[
            pl.BlockSpec((tm, D_MODEL), row),
            pl.BlockSpec((tm, D_MODEL), row),
            pl.BlockSpec((tm, D_MODEL), row),
            pl.BlockSpec((tm, 8), row),
            pl.BlockSpec((1, D_MODEL), lambda i: (0, 0)),
        ],
        out_specs=pl.BlockSpec((tm, D_MODEL), row),
        out_shape=jax.ShapeDtypeStruct((n, D_MODEL), F32),
        compiler_params=pltpu.CompilerParams(
            dimension_semantics=("arbitrary",), vmem_limit_bytes=VMEM_LIMIT),
        name="moe_combine",
    )(h, ya, yb, route, g)


def _dispatch_plan(route, tm):
    n = route.shape[0]
    experts = route[:, :2].astype(jnp.int32).reshape(-1)
    onehot = (experts[:, None] == jnp.arange(N_EXPERTS, dtype=jnp.int32)[None, :]).astype(jnp.int32)
    rank = jnp.take_along_axis(jnp.cumsum(onehot, axis=0) - onehot, experts[:, None], axis=1)[:, 0]
    counts = jnp.sum(onehot, axis=0)
    padded = ((counts + tm - 1) // tm) * tm
    starts = jnp.cumsum(padded) - padded
    slot = starts[experts] + rank
    p_rows = 2 * n + N_EXPERTS * tm
    src_tok = jnp.zeros((p_rows,), jnp.int32).at[slot].set(jnp.arange(2 * n, dtype=jnp.int32) // 2)
    ends = jnp.cumsum(padded)
    tile_start = jnp.arange(p_rows // tm, dtype=jnp.int32) * tm
    tile_expert = jnp.sum((tile_start[:, None] >= ends[None, :]).astype(jnp.int32), axis=1)
    used = (ends[-1] // tm).astype(jnp.int32)
    last_expert = jnp.take(tile_expert, jnp.maximum(used - 1, 0))
    tile_expert = jnp.where(tile_start // tm < used, tile_expert, last_expert).astype(jnp.int32)
    return src_tok, slot.reshape(n, 2), tile_expert, used.reshape(1)


def kernel(x, mem, positions, l0_attn_pre_g, l0_mem_g, l0_w_in, l0_w_mem_kv, l0_sinks, l0_w_out, l0_attn_post_g, l0_ffn_pre_g, l0_w_gate, l0_w_up, l0_w_down, l0_ffn_post_g, l1_attn_pre_g, l1_mem_g, l1_w_in, l1_w_mem_kv, l1_w_out, l1_attn_post_g, l1_ffn_pre_g, l1_w_router, l1_w_gate, l1_w_up, l1_w_down, l1_ffn_post_g):
    batch, seq, _ = x.shape
    n = batch * seq
    h = x.reshape(n, D_MODEL)
    mem2d = mem.reshape(-1, D_MODEL)
    row = lambda g: g.reshape(1, D_MODEL)

    kv0 = SELF_W
    kw = N_KV_HEADS_SWA * HEAD_DIM
    dup = lambda w: jnp.tile(w.reshape(D_MODEL, N_KV_HEADS_SWA, 1, HEAD_DIM), (1, 1, 2, 1)).reshape(D_MODEL, KV_DUP_W)
    w_in0 = jnp.concatenate(
        [l0_w_in[:, :kv0], dup(l0_w_in[:, kv0:kv0 + kw]), dup(l0_w_in[:, kv0 + kw:kv0 + 2 * kw]),
         l0_w_in[:, kv0 + 2 * kw:]], axis=1).astype(BF16)
    inv_freq = ROPE_THETA ** (-jnp.arange(0, HEAD_DIM, 2, dtype=F32) / HEAD_DIM)
    invf = jnp.tile(inv_freq, LANES // (HEAD_DIM // 2)).reshape(1, LANES)

    km0, vm0 = _mem_kv(mem2d, row(l0_mem_g), l0_w_mem_kv.astype(BF16))
    q, k, v, qc = _proj_swa(h, row(l0_attn_pre_g), w_in0, positions.reshape(n, 1), invf)
    h = _swa_layer(l0_sinks, q, k, v, qc, km0, vm0, l0_w_out.astype(BF16), row(l0_attn_post_g), h, batch, seq)
    h = _dense_ffn(h, row(l0_ffn_pre_g), l0_w_gate.astype(BF16), l0_w_up.astype(BF16),
                   l0_w_down.astype(BF16), row(l0_ffn_post_g))

    km1, vm1 = _mem_kv(mem2d, row(l1_mem_g), l1_w_mem_kv.astype(BF16))
    q, k, v, qc = _proj_sb(h, row(l1_attn_pre_g), l1_w_in.astype(BF16))
    wr_pad = jnp.pad(l1_w_router, ((0, 0), (0, LANES - N_EXPERTS)))
    h, xn, route = _sb_layer(q, k, v, qc, km1, vm1, l1_w_out.astype(BF16), row(l1_attn_post_g), h,
                             row(l1_ffn_pre_g), wr_pad, batch, seq)

    tm = 512
    src_tok, slot, tile_expert, used = _dispatch_plan(route, tm)
    xs = jnp.take(xn, src_tok, axis=0)
    ys = _moe_ffn(tile_expert, used, xs, l1_w_gate.astype(BF16), l1_w_up.astype(BF16),
                  l1_w_down.astype(BF16), tm)
    ya = jnp.take(ys, slot[:, 0], axis=0)
    yb = jnp.take(ys, slot[:, 1], axis=0)
    h = _combine(h, ya, yb, route, row(l1_ffn_post_g))
    return h.reshape(batch, seq, D_MODEL)
```

```python
import functools

import jax
import jax.numpy as jnp
from jax import lax
from jax.experimental import pallas as pl
from jax.experimental.pallas import tpu as pltpu

D_MODEL = 1024
HEAD_DIM = 64
N_SELF_HEADS = 12
N_KV_HEADS_SWA = 3
N_CROSS_HEADS = 4
SWA_WINDOW = 128
BLOCK = 128
ROPE_THETA = 10000.0
D_FF = 3584
N_EXPERTS = 8
RMS_EPS = 1e-6

SELF_W = N_SELF_HEADS * HEAD_DIM
CROSS_W = N_CROSS_HEADS * HEAD_DIM
KV_DUP_W = N_KV_HEADS_SWA * 2 * HEAD_DIM
LANES = 128
Q_SCALE = HEAD_DIM ** -0.5
NEG_BIG = -1e30
LOG2E = 1.4426950408889634
SB_DONE_LOG2 = 160.0

VMEM_LIMIT = 56 * 1024 * 1024

F32 = jnp.float32
BF16 = jnp.bfloat16


def _rms(xf, g):
    var = jnp.mean(xf * xf, axis=-1, keepdims=True)
    return xf * lax.rsqrt(var + RMS_EPS) * g


def _lane_iota(shape):
    return lax.broadcasted_iota(jnp.int32, shape, len(shape) - 1)


def _dot(a, b):
    return jnp.dot(a, b, preferred_element_type=F32)


def _dot_nt(a, b):
    return lax.dot_general(a, b, (((1,), (1,)), ((), ())), preferred_element_type=F32)


def _half_masked_pair(x2):
    lo = _lane_iota(x2.shape) < HEAD_DIM
    zero = jnp.zeros_like(x2)
    return jnp.concatenate([jnp.where(lo, x2, zero), jnp.where(lo, zero, x2)], axis=0)


def _merge_halves(o, r):
    lo = _lane_iota((r, LANES)) < HEAD_DIM
    return jnp.where(lo, o[:r], o[r:])


def _mem_kv_kernel(mem_ref, g_ref, w_ref, k_ref, v_ref):
    xn = _rms(mem_ref[...], g_ref[...]).astype(BF16)
    kv = _dot(xn, w_ref[...])
    k_ref[...] = kv[:, :CROSS_W].astype(BF16)
    v_ref[...] = kv[:, CROSS_W:].astype(BF16)


def _mem_kv(mem2d, g, w_bf16, tm=512):
    n = mem2d.shape[0]
    return pl.pallas_call(
        _mem_kv_kernel,
        grid=(n // tm,),
        in_specs=[
            pl.BlockSpec((tm, D_MODEL), lambda i: (i, 0)),
            pl.BlockSpec((1, D_MODEL), lambda i: (0, 0)),
            pl.BlockSpec((D_MODEL, 2 * CROSS_W), lambda i: (0, 0)),
        ],
        out_specs=[
            pl.BlockSpec((tm, CROSS_W), lambda i: (i, 0)),
            pl.BlockSpec((tm, CROSS_W), lambda i: (i, 0)),
        ],
        out_shape=[jax.ShapeDtypeStruct((n, CROSS_W), BF16)] * 2,
        compiler_params=pltpu.CompilerParams(
            dimension_semantics=("arbitrary",), vmem_limit_bytes=VMEM_LIMIT),
        name="mem_kv",
    )(mem2d, g, w_bf16)


def _rope_chunk(xc, cos, sin, first_half):
    rot = jnp.where(first_half, -pltpu.roll(xc, LANES - HEAD_DIM // 2, axis=1),
                    pltpu.roll(xc, HEAD_DIM // 2, axis=1))
    return xc * cos + rot * sin


def _proj_swa_kernel(h_ref, g_ref, w_ref, pos_ref, invf_ref, q_ref, k_ref, v_ref, qc_ref):
    xn = _rms(h_ref[...], g_ref[...]).astype(BF16)
    proj = _dot(xn, w_ref[...])
    ang = pos_ref[...].astype(F32) * invf_ref[...]
    cos = jnp.cos(ang)
    sin = jnp.sin(ang)
    first_half = (_lane_iota(cos.shape) % HEAD_DIM) < (HEAD_DIM // 2)
    for c in range(SELF_W // LANES):
        xc = proj[:, c * LANES:(c + 1) * LANES]
        q_ref[:, c * LANES:(c + 1) * LANES] = (
            _rope_chunk(xc, cos, sin, first_half) * Q_SCALE).astype(BF16)
    for c in range(KV_DUP_W // LANES):
        xc = proj[:, SELF_W + c * LANES:SELF_W + (c + 1) * LANES]
        k_ref[:, c * LANES:(c + 1) * LANES] = _rope_chunk(xc, cos, sin, first_half).astype(BF16)
    v0 = SELF_W + KV_DUP_W
    v_ref[...] = proj[:, v0:v0 + KV_DUP_W].astype(BF16)
    qc_ref[...] = (proj[:, v0 + KV_DUP_W:] * Q_SCALE).astype(BF16)


def _proj_swa(h, g, w_bf16, pos, invf, tm=512):
    n = h.shape[0]
    w_cols = w_bf16.shape[1]
    row = lambda i: (i, 0)
    fixed = lambda i: (0, 0)
    return pl.pallas_call(
        _proj_swa_kernel,
        grid=(n // tm,),
        in_specs=[
            pl.BlockSpec((tm, D_MODEL), row),
            pl.BlockSpec((1, D_MODEL), fixed),
            pl.BlockSpec((D_MODEL, w_cols), fixed),
            pl.BlockSpec((tm, 1), row),
            pl.BlockSpec((1, LANES), fixed),
        ],
        out_specs=[
            pl.BlockSpec((tm, SELF_W), row),
            pl.BlockSpec((tm, KV_DUP_W), row),
            pl.BlockSpec((tm, KV_DUP_W), row),
            pl.BlockSpec((tm, CROSS_W), row),
        ],
        out_shape=[
            jax.ShapeDtypeStruct((n, SELF_W), BF16),
            jax.ShapeDtypeStruct((n, KV_DUP_W), BF16),
            jax.ShapeDtypeStruct((n, KV_DUP_W), BF16),
            jax.ShapeDtypeStruct((n, CROSS_W), BF16),
        ],
        compiler_params=pltpu.CompilerParams(
            dimension_semantics=("arbitrary",), vmem_limit_bytes=VMEM_LIMIT),
        name="proj_swa",
    )(h, g, w_bf16, pos, invf)


def _proj_sb_kernel(h_ref, g_ref, w_ref, q_ref, k_ref, v_ref, qc_ref):
    xn = _rms(h_ref[...], g_ref[...]).astype(BF16)
    proj = _dot(xn, w_ref[...])
    q_ref[...] = (proj[:, :SELF_W] * (Q_SCALE * LOG2E)).astype(BF16)
    k_ref[...] = proj[:, SELF_W:2 * SELF_W].astype(BF16)
    v_ref[...] = proj[:, 2 * SELF_W:3 * SELF_W].astype(BF16)
    qc_ref[...] = (proj[:, 3 * SELF_W:] * Q_SCALE).astype(BF16)


def _proj_sb(h, g, w_bf16, tm=512):
    n = h.shape[0]
    row = lambda i: (i, 0)
    fixed = lambda i: (0, 0)
    return pl.pallas_call(
        _proj_sb_kernel,
        grid=(n // tm,),
        in_specs=[
            pl.BlockSpec((tm, D_MODEL), row),
            pl.BlockSpec((1, D_MODEL), fixed),
            pl.BlockSpec((D_MODEL, w_bf16.shape[1]), fixed),
        ],
        out_specs=[
            pl.BlockSpec((tm, SELF_W), row),
            pl.BlockSpec((tm, SELF_W), row),
            pl.BlockSpec((tm, SELF_W), row),
            pl.BlockSpec((tm, CROSS_W), row),
        ],
        out_shape=[
            jax.ShapeDtypeStruct((n, SELF_W), BF16),
            jax.ShapeDtypeStruct((n, SELF_W), BF16),
            jax.ShapeDtypeStruct((n, SELF_W), BF16),
            jax.ShapeDtypeStruct((n, CROSS_W), BF16),
        ],
        compiler_params=pltpu.CompilerParams(
            dimension_semantics=("arbitrary",), vmem_limit_bytes=VMEM_LIMIT),
        name="proj_sb",
    )(h, g, w_bf16)


def _cross_attention(qc_ref, km_ref, vm_ref, mixed_ref, tq):
    for p in range(CROSS_W // LANES):
        sl = slice(p * LANES, (p + 1) * LANES)
        qs = _half_masked_pair(qc_ref[:, sl])
        s = _dot_nt(qs, km_ref[:, sl])
        m = jnp.max(s, axis=-1, keepdims=True)
        e = jnp.exp(s - m)
        probs = (e / jnp.sum(e, axis=-1, keepdims=True)).astype(BF16)
        o = _dot(probs, vm_ref[:, sl])
        mixed_ref[:, SELF_W + p * LANES:SELF_W + (p + 1) * LANES] = _merge_halves(o, tq).astype(BF16)


def _out_proj_residual(mixed_ref, wo_ref, g_ref, h_ref):
    mixed = _dot(mixed_ref[...], wo_ref[...])
    return h_ref[...] + _rms(mixed, g_ref[...])


def _swa_kernel(sinks_ref, q_ref, k_ref, v_ref, qc_ref, km_ref, vm_ref, wo_ref, g_ref, h_ref,
                o_ref, mixed_ref, *, tq):
    i = pl.program_id(1)
    blocks = tq // BLOCK
    group_rows = 4 * BLOCK

    qi = lax.broadcasted_iota(jnp.int32, (group_rows, 2 * BLOCK), 0) % BLOCK
    kj = lax.broadcasted_iota(jnp.int32, (group_rows, 2 * BLOCK), 1)
    diff = qi + BLOCK - kj
    in_win = (diff >= 0) & (diff < SWA_WINDOW)
    row_head = lax.broadcasted_iota(jnp.int32, (group_rows, 1), 0) // BLOCK

    for blk in range(blocks):
        n = i * blocks + blk
        r0 = blk * BLOCK
        cur0 = pl.multiple_of(n * BLOCK, BLOCK)
        prev0 = pl.multiple_of(jnp.maximum(n - 1, 0) * BLOCK, BLOCK)
        mask = in_win & ((kj >= BLOCK) | (n > 0))
        for g in range(N_KV_HEADS_SWA):
            gs = slice(g * LANES, (g + 1) * LANES)
            kcat = jnp.concatenate([k_ref[pl.ds(prev0, BLOCK), gs], k_ref[pl.ds(cur0, BLOCK), gs]], axis=0)
            vcat = jnp.concatenate([v_ref[pl.ds(prev0, BLOCK), gs], v_ref[pl.ds(cur0, BLOCK), gs]], axis=0)
            c0 = g * 2 * LANES
            qs = jnp.concatenate(
                [_half_masked_pair(q_ref[r0:r0 + BLOCK, c0:c0 + LANES]),
                 _half_masked_pair(q_ref[r0:r0 + BLOCK, c0 + LANES:c0 + 2 * LANES])], axis=0)
            s = jnp.where(mask, _dot_nt(qs, kcat), NEG_BIG)
            sink = jnp.zeros((group_rows, 1), F32)
            for hh in range(4):
                sink = jnp.where(row_head == hh, sinks_ref[4 * g + hh], sink)
            m = jnp.maximum(jnp.max(s, axis=-1, keepdims=True), sink)
            e = jnp.exp(s - m)
            denom = jnp.sum(e, axis=-1, keepdims=True) + jnp.exp(sink - m)
            probs = (e / denom).astype(BF16)
            o = _dot(probs, vcat)
            mixed_ref[r0:r0 + BLOCK, c0:c0 + LANES] = _merge_halves(o[:2 * BLOCK], BLOCK).astype(BF16)
            mixed_ref[r0:r0 + BLOCK, c0 + LANES:c0 + 2 * LANES] = _merge_halves(o[2 * BLOCK:], BLOCK).astype(BF16)

    _cross_attention(qc_ref, km_ref, vm_ref, mixed_ref, tq)
    o_ref[...] = _out_proj_residual(mixed_ref, wo_ref, g_ref, h_ref)


def _swa_layer(sinks, q, k, v, qc, km, vm, wo_bf16, g, h, batch, seq, tq=512):
    n = h.shape[0]
    mem_len = km.shape[0] // batch
    tiles = seq // tq
    row = lambda b, i, s: (b * tiles + i, 0)
    per_b = lambda b, i, s: (b, 0)
    fixed = lambda b, i, s: (0, 0)
    grid_spec = pltpu.PrefetchScalarGridSpec(
        num_scalar_prefetch=1,
        grid=(batch, tiles),
        in_specs=[
            pl.BlockSpec((tq, SELF_W), row),
            pl.BlockSpec((seq, KV_DUP_W), per_b),
            pl.BlockSpec((seq, KV_DUP_W), per_b),
            pl.BlockSpec((tq, CROSS_W), row),
            pl.BlockSpec((mem_len, CROSS_W), per_b),
            pl.BlockSpec((mem_len, CROSS_W), per_b),
            pl.BlockSpec((D_MODEL, D_MODEL), fixed),
            pl.BlockSpec((1, D_MODEL), fixed),
            pl.BlockSpec((tq, D_MODEL), row),
        ],
        out_specs=pl.BlockSpec((tq, D_MODEL), row),
        scratch_shapes=[pltpu.VMEM((tq, D_MODEL), BF16)],
    )
    return pl.pallas_call(
        functools.partial(_swa_kernel, tq=tq),
        grid_spec=grid_spec,
        out_shape=jax.ShapeDtypeStruct((n, D_MODEL), F32),
        compiler_params=pltpu.CompilerParams(
            dimension_semantics=("arbitrary", "arbitrary"), vmem_limit_bytes=VMEM_LIMIT),
        name="swa_layer",
    )(sinks, q, k, v, qc, km, vm, wo_bf16, g, h)


def _sb_kernel(q_ref, k_ref, v_ref, qc_ref, km_ref, vm_ref, wo_ref, g_ref, h_ref, g2_ref, wr_ref,
               o_ref, xn_ref, route_ref, mixed_ref, qs_ref, acc_ref, carry_ref, *, tq):
    i = pl.program_id(1)
    tk = tq
    rows = 2 * tq
    pairs = SELF_W // LANES

    r_idx = lax.broadcasted_iota(jnp.int32, (rows, tk), 0) % tq
    c_idx = lax.broadcasted_iota(jnp.int32, (rows, tk), 1)
    causal = c_idx < r_idx
    kr = lax.broadcasted_iota(jnp.int32, (tk, tk), 0)
    kc = lax.broadcasted_iota(jnp.int32, (tk, tk), 1)
    suffix = jnp.where(kr >= kc, 1.0, 0.0).astype(BF16)

    for p in range(pairs):
        qs_ref[p] = _half_masked_pair(q_ref[:, p * LANES:(p + 1) * LANES])

    def block(j, diagonal):
        k0 = pl.multiple_of(j * tk, tk)
        for p in range(pairs):
            sl = slice(p * LANES, (p + 1) * LANES)
            z = _dot_nt(qs_ref[p], k_ref[pl.ds(k0, tk), sl])
            sp = jnp.maximum(z, 0.0) + jnp.log(1.0 + jnp.exp2(-jnp.abs(z))) * LOG2E
            if diagonal:
                sp = jnp.where(causal, sp, 0.0)
            s_incl = _dot(sp.astype(BF16), suffix)
            total = jnp.broadcast_to(s_incl[:, 0:1], (rows, LANES))
            if diagonal:
                w = jnp.where(causal, jnp.exp2(z - s_incl), 0.0)
                acc_ref[p] = _dot(w.astype(BF16), v_ref[pl.ds(k0, tk), sl])
                carry_ref[p] = total
            else:
                c = carry_ref[p]
                w = jnp.exp2(z - s_incl - jnp.concatenate([c] * (tk // LANES), axis=1))
                acc_ref[p] += _dot(w.astype(BF16), v_ref[pl.ds(k0, tk), sl])
                carry_ref[p] = c + total

    block(i, True)

    def unfinished(state):
        t, cmin = state
        return (t < i) & (cmin < SB_DONE_LOG2)

    def earlier_block(state):
        t, _ = state
        block(i - 1 - t, False)
        return t + 1, jnp.min(carry_ref[...])

    lax.while_loop(unfinished, earlier_block, (jnp.int32(0), jnp.min(carry_ref[...])))

    for p in range(pairs):
        mixed_ref[:, p * LANES:(p + 1) * LANES] = _merge_halves(acc_ref[p], tq).astype(BF16)

    _cross_attention(qc_ref, km_ref, vm_ref, mixed_ref, tq)
    h1 = _out_proj_residual(mixed_ref, wo_ref, g_ref, h_ref)
    o_ref[...] = h1

    xn = _rms(h1, g2_ref[...])
    xh = xn.astype(BF16)
    xn_ref[...] = xh
    xl = (xn - xh.astype(F32)).astype(BF16)
    wr = wr_ref[...]
    hi = _dot(xh, wr)
    logits = hi[:, :LANES] + hi[:, LANES:] + _dot(xl, wr[:, :LANES])
    lane = _lane_iota(logits.shape)
    l1 = jnp.where(lane < N_EXPERTS, logits, NEG_BIG)
    m1 = jnp.max(l1, axis=-1, keepdims=True)
    i1 = jnp.min(jnp.where(l1 == m1, lane, LANES), axis=-1, keepdims=True)
    l2 = jnp.where(lane == i1, NEG_BIG, l1)
    m2 = jnp.max(l2, axis=-1, keepdims=True)
    i2 = jnp.min(jnp.where(l2 == m2, lane, LANES), axis=-1, keepdims=True)
    e2 = jnp.exp(m2 - m1)
    w1 = 1.0 / (1.0 + e2)
    w2 = e2 / (1.0 + e2)
    lane8 = _lane_iota((tq, 8))
    route_ref[...] = jnp.where(lane8 == 0, i1.astype(F32),
                     jnp.where(lane8 == 1, i2.astype(F32),
                     jnp.where(lane8 == 2, w1, jnp.where(lane8 == 3, w2, 0.0))))


def _sb_layer(q, k, v, qc, km, vm, wo_bf16, g, h, g2, wr_split, batch, seq, tq=256):
    n = h.shape[0]
    mem_len = km.shape[0] // batch
    tiles = seq // tq
    row = lambda b, i: (b * tiles + i, 0)
    per_b = lambda b, i: (b, 0)
    fixed = lambda b, i: (0, 0)
    return pl.pallas_call(
        functools.partial(_sb_kernel, tq=tq),
        grid=(batch, tiles),
        in_specs=[
            pl.BlockSpec((tq, SELF_W), row),
            pl.BlockSpec((seq, SELF_W), per_b),
            pl.BlockSpec((seq, SELF_W), per_b),
            pl.BlockSpec((tq, CROSS_W), row),
            pl.BlockSpec((mem_len, CROSS_W), per_b),
            pl.BlockSpec((mem_len, CROSS_W), per_b),
            pl.BlockSpec((D_MODEL, D_MODEL), fixed),
            pl.BlockSpec((1, D_MODEL), fixed),
            pl.BlockSpec((tq, D_MODEL), row),
            pl.BlockSpec((1, D_MODEL), fixed),
            pl.BlockSpec((D_MODEL, 2 * LANES), fixed),
        ],
        out_specs=[
            pl.BlockSpec((tq, D_MODEL), row),
            pl.BlockSpec((tq, D_MODEL), row),
            pl.BlockSpec((tq, 8), row),
        ],
        out_shape=[
            jax.ShapeDtypeStruct((n, D_MODEL), F32),
            jax.ShapeDtypeStruct((n, D_MODEL), BF16),
            jax.ShapeDtypeStruct((n, 8), F32),
        ],
        scratch_shapes=[
            pltpu.VMEM((tq, D_MODEL), BF16),
            pltpu.VMEM((SELF_W // LANES, 2 * tq, LANES), BF16),
            pltpu.VMEM((SELF_W // LANES, 2 * tq, LANES), F32),
            pltpu.VMEM((SELF_W // LANES, 2 * tq, LANES), F32),
        ],
        compiler_params=pltpu.CompilerParams(
            dimension_semantics=("arbitrary", "arbitrary"), vmem_limit_bytes=VMEM_LIMIT),
        name="sb_layer",
    )(q, k, v, qc, km, vm, wo_bf16, g, h, g2, wr_split)


def _swiglu_part(x_bf16, wg_ref, wu_ref, wd_ref):
    gate = _dot(x_bf16, wg_ref[...].astype(BF16))
    up = _dot(x_bf16, wu_ref[...].astype(BF16))
    act = (gate * jax.nn.sigmoid(gate) * up).astype(BF16)
    return _dot(act, wd_ref[...].astype(BF16))


def _dense_ffn_kernel(h_ref, g1_ref, wg_ref, wu_ref, wd_ref, g2_ref, o_ref, xn_ref, acc_ref):
    f = pl.program_id(1)

    @pl.when(f == 0)
    def _():
        xn_ref[...] = _rms(h_ref[...], g1_ref[...]).astype(BF16)

    part = _swiglu_part(xn_ref[...], wg_ref, wu_ref, wd_ref)

    @pl.when(f == 0)
    def _():
        acc_ref[...] = part

    @pl.when(f > 0)
    def _():
        acc_ref[...] += part

    @pl.when(f == pl.num_programs(1) - 1)
    def _():
        o_ref[...] = h_ref[...] + _rms(acc_ref[...], g2_ref[...])


def _dense_ffn(h, g1, wg, wu, wd, g2, tm=1024, tf=512):
    n = h.shape[0]
    row = lambda i, f: (i, 0)
    fixed = lambda i, f: (0, 0)
    return pl.pallas_call(
        _dense_ffn_kernel,
        grid=(n // tm, D_FF // tf),
        in_specs=[
            pl.BlockSpec((tm, D_MODEL), row),
            pl.BlockSpec((1, D_MODEL), fixed),
            pl.BlockSpec((D_MODEL, tf), lambda i, f: (0, f)),
            pl.BlockSpec((D_MODEL, tf), lambda i, f: (0, f)),
            pl.BlockSpec((tf, D_MODEL), lambda i, f: (f, 0)),
            pl.BlockSpec((1, D_MODEL), fixed),
        ],
        out_specs=pl.BlockSpec((tm, D_MODEL), row),
        out_shape=jax.ShapeDtypeStruct((n, D_MODEL), F32),
        scratch_shapes=[pltpu.VMEM((tm, D_MODEL), BF16), pltpu.VMEM((tm, D_MODEL), F32)],
        compiler_params=pltpu.CompilerParams(
            dimension_semantics=("arbitrary", "arbitrary"), vmem_limit_bytes=VMEM_LIMIT),
        name="dense_ffn",
    )(h, g1, wg, wu, wd, g2)


def _moe_ffn_kernel(te_ref, used_ref, xs_ref, wg_ref, wu_ref, wd_ref, o_ref, acc_ref):
    i = pl.program_id(0)
    f = pl.program_id(1)

    @pl.when(i < used_ref[0])
    def _():
        part = _swiglu_part(xs_ref[...], wg_ref, wu_ref, wd_ref)

        @pl.when(f == 0)
        def _():
            acc_ref[...] = part

        @pl.when(f > 0)
        def _():
            acc_ref[...] += part

        @pl.when(f == pl.num_programs(1) - 1)
        def _():
            o_ref[...] = acc_ref[...]

    @pl.when((i >= used_ref[0]) & (f == pl.num_programs(1) - 1))
    def _():
        o_ref[...] = jnp.zeros_like(o_ref)


def _moe_ffn(tile_expert, used, xs, wg, wu, wd, tm, tf=512):
    p_rows = xs.shape[0]
    nf = D_FF // tf

    def f_eff(i, f, used_ref):
        return jnp.where(i < used_ref[0], f, nf - 1)

    grid_spec = pltpu.PrefetchScalarGridSpec(
        num_scalar_prefetch=2,
        grid=(p_rows // tm, nf),
        in_specs=[
            pl.BlockSpec((tm, D_MODEL), lambda i, f, te, u: (i, 0)),
            pl.BlockSpec((None, D_MODEL, tf), lambda i, f, te, u: (te[i], 0, f_eff(i, f, u))),
            pl.BlockSpec((None, D_MODEL, tf), lambda i, f, te, u: (te[i], 0, f_eff(i, f, u))),
            pl.BlockSpec((None, tf, D_MODEL), lambda i, f, te, u: (te[i], f_eff(i, f, u), 0)),
        ],
        out_specs=pl.BlockSpec((tm, D_MODEL), lambda i, f, te, u: (i, 0)),
        scratch_shapes=[pltpu.VMEM((tm, D_MODEL), F32)],
    )
    return pl.pallas_call(
        _moe_ffn_kernel,
        grid_spec=grid_spec,
        out_shape=jax.ShapeDtypeStruct((p_rows, D_MODEL), F32),
        compiler_params=pltpu.CompilerParams(
            dimension_semantics=("arbitrary", "arbitrary"), vmem_limit_bytes=VMEM_LIMIT),
        name="moe_ffn",
    )(tile_expert, used, xs, wg, wu, wd)


def _combine_kernel(h_ref, ya_ref, yb_ref, route_ref, g_ref, o_ref):
    r = route_ref[...]
    y = r[:, 2:3] * ya_ref[...] + r[:, 3:4] * yb_ref[...]
    o_ref[...] = h_ref[...] + _rms(y, g_ref[...])


def _combine(h, ya, yb, route, g, tm=512):
    n = h.shape[0]
    row = lambda i: (i, 0)
    return pl.pallas_call(
        _combine_kernel,
        grid=(n // tm,),
        in_specs=[
            pl.BlockSpec((tm, D_MODEL), row),
            pl.BlockSpec((tm, D_MODEL), row),
            pl.BlockSpec((tm, D_MODEL), row),
            pl.BlockSpec((tm, 8), row),
            pl.BlockSpec((1, D_MODEL), lambda i: (0, 0)),
        ],
        out_specs=pl.BlockSpec((tm, D_MODEL), row),
        out_shape=jax.ShapeDtypeStruct((n, D_MODEL), F32),
        compiler_params=pltpu.CompilerParams(
            dimension_semantics=("arbitrary",), vmem_limit_bytes=VMEM_LIMIT),
        name="moe_combine",
    )(h, ya, yb, route, g)


def _dispatch_plan(route, tm):
    n = route.shape[0]
    blk = 256
    ex = route[:, :2].astype(jnp.int32)
    eids = jnp.arange(N_EXPERTS, dtype=jnp.int32)[None, :]
    member = (ex[:, 0:1] == eids) | (ex[:, 1:2] == eids)
    mb = member.reshape(n // blk, blk, N_EXPERTS).astype(BF16)
    before = (jnp.arange(blk)[:, None] > jnp.arange(blk)[None, :]).astype(BF16)
    local = jnp.einsum('ij,bjk->bik', before, mb, preferred_element_type=F32)
    tot = jnp.sum(mb.astype(F32), axis=1)
    base = jnp.cumsum(tot, axis=0) - tot
    rank = (local + base[:, None, :]).reshape(n, N_EXPERTS).astype(jnp.int32)
    counts = jnp.sum(tot, axis=0).astype(jnp.int32)
    padded = ((counts + tm - 1) // tm) * tm
    ends = jnp.cumsum(padded)
    slot_all = (ends - padded)[None, :] + rank
    slot = jnp.stack([jnp.sum(jnp.where(ex[:, k:k + 1] == eids, slot_all, 0), axis=1) for k in range(2)], axis=1)
    p_rows = 2 * n + N_EXPERTS * tm
    tok = jnp.broadcast_to(jnp.arange(n, dtype=jnp.int32)[:, None], (n, 2))
    src_tok = jnp.zeros((p_rows,), jnp.int32).at[slot.reshape(-1)].set(tok.reshape(-1))
    tile_start = jnp.arange(p_rows // tm, dtype=jnp.int32) * tm
    tile_expert = jnp.sum((tile_start[:, None] >= ends[None, :]).astype(jnp.int32), axis=1)
    used = (ends[-1] // tm).astype(jnp.int32)
    last_expert = jnp.take(tile_expert, jnp.maximum(used - 1, 0))
    tile_expert = jnp.where(tile_start // tm < used, tile_expert, last_expert).astype(jnp.int32)
    return src_tok, slot, tile_expert, used.reshape(1)


def kernel(x, mem, positions, l0_attn_pre_g, l0_mem_g, l0_w_in, l0_w_mem_kv, l0_sinks, l0_w_out, l0_attn_post_g, l0_ffn_pre_g, l0_w_gate, l0_w_up, l0_w_down, l0_ffn_post_g, l1_attn_pre_g, l1_mem_g, l1_w_in, l1_w_mem_kv, l1_w_out, l1_attn_post_g, l1_ffn_pre_g, l1_w_router, l1_w_gate, l1_w_up, l1_w_down, l1_ffn_post_g):
    batch, seq, _ = x.shape
    n = batch * seq
    h = x.reshape(n, D_MODEL)
    mem2d = mem.reshape(-1, D_MODEL)
    row = lambda g: g.reshape(1, D_MODEL)

    kv0 = SELF_W
    kw = N_KV_HEADS_SWA * HEAD_DIM
    dup = lambda w: jnp.tile(w.reshape(D_MODEL, N_KV_HEADS_SWA, 1, HEAD_DIM), (1, 1, 2, 1)).reshape(D_MODEL, KV_DUP_W)
    w_in0 = jnp.concatenate(
        [l0_w_in[:, :kv0], dup(l0_w_in[:, kv0:kv0 + kw]), dup(l0_w_in[:, kv0 + kw:kv0 + 2 * kw]),
         l0_w_in[:, kv0 + 2 * kw:]], axis=1).astype(BF16)
    inv_freq = ROPE_THETA ** (-jnp.arange(0, HEAD_DIM, 2, dtype=F32) / HEAD_DIM)
    invf = jnp.tile(inv_freq, LANES // (HEAD_DIM // 2)).reshape(1, LANES)

    km0, vm0 = _mem_kv(mem2d, row(l0_mem_g), l0_w_mem_kv.astype(BF16))
    q, k, v, qc = _proj_swa(h, row(l0_attn_pre_g), w_in0, positions.reshape(n, 1), invf)
    h = _swa_layer(l0_sinks, q, k, v, qc, km0, vm0, l0_w_out.astype(BF16), row(l0_attn_post_g), h, batch, seq)
    h = _dense_ffn(h, row(l0_ffn_pre_g), l0_w_gate, l0_w_up, l0_w_down, row(l0_ffn_post_g))

    km1, vm1 = _mem_kv(mem2d, row(l1_mem_g), l1_w_mem_kv.astype(BF16))
    q, k, v, qc = _proj_sb(h, row(l1_attn_pre_g), l1_w_in.astype(BF16))
    wr_pad = jnp.pad(l1_w_router, ((0, 0), (0, LANES - N_EXPERTS)))
    wr_hi = wr_pad.astype(BF16)
    wr_split = jnp.concatenate([wr_hi, (wr_pad - wr_hi.astype(F32)).astype(BF16)], axis=1)
    h, xn, route = _sb_layer(q, k, v, qc, km1, vm1, l1_w_out.astype(BF16), row(l1_attn_post_g), h,
                             row(l1_ffn_pre_g), wr_split, batch, seq)

    tm = 1024
    src_tok, slot, tile_expert, used = _dispatch_plan(route, tm)
    xs = jnp.take(xn, src_tok, axis=0)
    ys = _moe_ffn(tile_expert, used, xs, l1_w_gate, l1_w_up, l1_w_down, tm)
    ya = jnp.take(ys, slot[:, 0], axis=0)
    yb = jnp.take(ys, slot[:, 1], axis=0)
    h = _combine(h, ya, yb, route, row(l1_ffn_post_g))
    return h.reshape(batch, seq, D_MODEL)
```

```python
import functools

import jax
import jax.numpy as jnp
from jax import lax
from jax.experimental import pallas as pl
from jax.experimental.pallas import tpu as pltpu

D_MODEL = 1024
HEAD_DIM = 64
N_SELF_HEADS = 12
N_KV_HEADS_SWA = 3
N_CROSS_HEADS = 4
SWA_WINDOW = 128
BLOCK = 128
ROPE_THETA = 10000.0
D_FF = 3584
N_EXPERTS = 8
RMS_EPS = 1e-6

SELF_W = N_SELF_HEADS * HEAD_DIM
CROSS_W = N_CROSS_HEADS * HEAD_DIM
KV_DUP_W = N_KV_HEADS_SWA * 2 * HEAD_DIM
LANES = 128
Q_SCALE = HEAD_DIM ** -0.5
NEG_BIG = -1e30
LOG2E = 1.4426950408889634
SB_DONE_LOG2 = 160.0

VMEM_LIMIT = 56 * 1024 * 1024

F32 = jnp.float32
BF16 = jnp.bfloat16


def _rms(xf, g):
    var = jnp.mean(xf * xf, axis=-1, keepdims=True)
    return xf * lax.rsqrt(var + RMS_EPS) * g


def _lane_iota(shape):
    return lax.broadcasted_iota(jnp.int32, shape, len(shape) - 1)


def _dot(a, b):
    return jnp.dot(a, b, preferred_element_type=F32)


def _dot_nt(a, b):
    return lax.dot_general(a, b, (((1,), (1,)), ((), ())), preferred_element_type=F32)


def _half_masked_pair(x2):
    lo = _lane_iota(x2.shape) < HEAD_DIM
    zero = jnp.zeros_like(x2)
    return jnp.concatenate([jnp.where(lo, x2, zero), jnp.where(lo, zero, x2)], axis=0)


def _merge_halves(o, r):
    lo = _lane_iota((r, LANES)) < HEAD_DIM
    return jnp.where(lo, o[:r], o[r:])


def _mem_kv_kernel(mem_ref, g_ref, w_ref, k_ref, v_ref):
    xn = _rms(mem_ref[...], g_ref[...]).astype(BF16)
    kv = _dot(xn, w_ref[...])
    k_ref[...] = kv[:, :CROSS_W].astype(BF16)
    v_ref[...] = kv[:, CROSS_W:].astype(BF16)


def _mem_kv(mem2d, g, w_bf16, tm=512):
    n = mem2d.shape[0]
    return pl.pallas_call(
        _mem_kv_kernel,
        grid=(n // tm,),
        in_specs=[
            pl.BlockSpec((tm, D_MODEL), lambda i: (i, 0)),
            pl.BlockSpec((1, D_MODEL), lambda i: (0, 0)),
            pl.BlockSpec((D_MODEL, 2 * CROSS_W), lambda i: (0, 0)),
        ],
        out_specs=[
            pl.BlockSpec((tm, CROSS_W), lambda i: (i, 0)),
            pl.BlockSpec((tm, CROSS_W), lambda i: (i, 0)),
        ],
        out_shape=[jax.ShapeDtypeStruct((n, CROSS_W), BF16)] * 2,
        compiler_params=pltpu.CompilerParams(
            dimension_semantics=("arbitrary",), vmem_limit_bytes=VMEM_LIMIT),
        name="mem_kv",
    )(mem2d, g, w_bf16)


def _rope_chunk(xc, cos, sin, first_half):
    rot = jnp.where(first_half, -pltpu.roll(xc, LANES - HEAD_DIM // 2, axis=1),
                    pltpu.roll(xc, HEAD_DIM // 2, axis=1))
    return xc * cos + rot * sin


def _proj_swa_kernel(h_ref, g_ref, w_ref, pos_ref, invf_ref, q_ref, k_ref, v_ref, qc_ref):
    xn = _rms(h_ref[...], g_ref[...]).astype(BF16)
    proj = _dot(xn, w_ref[...])
    ang = pos_ref[...].astype(F32) * invf_ref[...]
    cos = jnp.cos(ang)
    sin = jnp.sin(ang)
    first_half = (_lane_iota(cos.shape) % HEAD_DIM) < (HEAD_DIM // 2)
    for c in range(SELF_W // LANES):
        xc = proj[:, c * LANES:(c + 1) * LANES]
        q_ref[:, c * LANES:(c + 1) * LANES] = (
            _rope_chunk(xc, cos, sin, first_half) * Q_SCALE).astype(BF16)
    for c in range(KV_DUP_W // LANES):
        xc = proj[:, SELF_W + c * LANES:SELF_W + (c + 1) * LANES]
        k_ref[:, c * LANES:(c + 1) * LANES] = _rope_chunk(xc, cos, sin, first_half).astype(BF16)
    v0 = SELF_W + KV_DUP_W
    v_ref[...] = proj[:, v0:v0 + KV_DUP_W].astype(BF16)
    qc_ref[...] = (proj[:, v0 + KV_DUP_W:] * Q_SCALE).astype(BF16)


def _proj_swa(h, g, w_bf16, pos, invf, tm=512):
    n = h.shape[0]
    w_cols = w_bf16.shape[1]
    row = lambda i: (i, 0)
    fixed = lambda i: (0, 0)
    return pl.pallas_call(
        _proj_swa_kernel,
        grid=(n // tm,),
        in_specs=[
            pl.BlockSpec((tm, D_MODEL), row),
            pl.BlockSpec((1, D_MODEL), fixed),
            pl.BlockSpec((D_MODEL, w_cols), fixed),
            pl.BlockSpec((tm, 1), row),
            pl.BlockSpec((1, LANES), fixed),
        ],
        out_specs=[
            pl.BlockSpec((tm, SELF_W), row),
            pl.BlockSpec((tm, KV_DUP_W), row),
            pl.BlockSpec((tm, KV_DUP_W), row),
            pl.BlockSpec((tm, CROSS_W), row),
        ],
        out_shape=[
            jax.ShapeDtypeStruct((n, SELF_W), BF16),
            jax.ShapeDtypeStruct((n, KV_DUP_W), BF16),
            jax.ShapeDtypeStruct((n, KV_DUP_W), BF16),
            jax.ShapeDtypeStruct((n, CROSS_W), BF16),
        ],
        compiler_params=pltpu.CompilerParams(
            dimension_semantics=("arbitrary",), vmem_limit_bytes=VMEM_LIMIT),
        name="proj_swa",
    )(h, g, w_bf16, pos, invf)


def _proj_sb_kernel(h_ref, g_ref, w_ref, q_ref, k_ref, v_ref, qc_ref):
    xn = _rms(h_ref[...], g_ref[...]).astype(BF16)
    proj = _dot(xn, w_ref[...])
    q_ref[...] = (proj[:, :SELF_W] * (Q_SCALE * LOG2E)).astype(BF16)
    k_ref[...] = proj[:, SELF_W:2 * SELF_W].astype(BF16)
    v_ref[...] = proj[:, 2 * SELF_W:3 * SELF_W].astype(BF16)
    qc_ref[...] = (proj[:, 3 * SELF_W:] * Q_SCALE).astype(BF16)


def _proj_sb(h, g, w_bf16, tm=512):
    n = h.shape[0]
    row = lambda i: (i, 0)
    fixed = lambda i: (0, 0)
    return pl.pallas_call(
        _proj_sb_kernel,
        grid=(n // tm,),
        in_specs=[
            pl.BlockSpec((tm, D_MODEL), row),
            pl.BlockSpec((1, D_MODEL), fixed),
            pl.BlockSpec((D_MODEL, w_bf16.shape[1]), fixed),
        ],
        out_specs=[
            pl.BlockSpec((tm, SELF_W), row),
            pl.BlockSpec((tm, SELF_W), row),
            pl.BlockSpec((tm, SELF_W), row),
            pl.BlockSpec((tm, CROSS_W), row),
        ],
        out_shape=[
            jax.ShapeDtypeStruct((n, SELF_W), BF16),
            jax.ShapeDtypeStruct((n, SELF_W), BF16),
            jax.ShapeDtypeStruct((n, SELF_W), BF16),
            jax.ShapeDtypeStruct((n, CROSS_W), BF16),
        ],
        compiler_params=pltpu.CompilerParams(
            dimension_semantics=("arbitrary",), vmem_limit_bytes=VMEM_LIMIT),
        name="proj_sb",
    )(h, g, w_bf16)


def _cross_attention(qc_ref, km_ref, vm_ref, mixed_ref, tq):
    for p in range(CROSS_W // LANES):
        sl = slice(p * LANES, (p + 1) * LANES)
        qs = _half_masked_pair(qc_ref[:, sl])
        s = _dot_nt(qs, km_ref[:, sl])
        m = jnp.max(s, axis=-1, keepdims=True)
        e = jnp.exp(s - m)
        probs = (e / jnp.sum(e, axis=-1, keepdims=True)).astype(BF16)
        o = _dot(probs, vm_ref[:, sl])
        mixed_ref[:, SELF_W + p * LANES:SELF_W + (p + 1) * LANES] = _merge_halves(o, tq).astype(BF16)


def _out_proj_residual(mixed_ref, wo_ref, g_ref, h_ref):
    mixed = _dot(mixed_ref[...], wo_ref[...])
    return h_ref[...] + _rms(mixed, g_ref[...])


def _swa_kernel(sinks_ref, q_ref, k_ref, v_ref, qc_ref, km_ref, vm_ref, wo_ref, g_ref, h_ref,
                o_ref, mixed_ref, *, tq):
    i = pl.program_id(1)
    blocks = tq // BLOCK
    group_rows = 4 * BLOCK

    qi = lax.broadcasted_iota(jnp.int32, (group_rows, 2 * BLOCK), 0) % BLOCK
    kj = lax.broadcasted_iota(jnp.int32, (group_rows, 2 * BLOCK), 1)
    diff = qi + BLOCK - kj
    in_win = (diff >= 0) & (diff < SWA_WINDOW)
    row_head = lax.broadcasted_iota(jnp.int32, (group_rows, 1), 0) // BLOCK

    for blk in range(blocks):
        n = i * blocks + blk
        r0 = blk * BLOCK
        cur0 = pl.multiple_of(n * BLOCK, BLOCK)
        prev0 = pl.multiple_of(jnp.maximum(n - 1, 0) * BLOCK, BLOCK)
        mask = in_win & ((kj >= BLOCK) | (n > 0))
        for g in range(N_KV_HEADS_SWA):
            gs = slice(g * LANES, (g + 1) * LANES)
            kcat = jnp.concatenate([k_ref[pl.ds(prev0, BLOCK), gs], k_ref[pl.ds(cur0, BLOCK), gs]], axis=0)
            vcat = jnp.concatenate([v_ref[pl.ds(prev0, BLOCK), gs], v_ref[pl.ds(cur0, BLOCK), gs]], axis=0)
            c0 = g * 2 * LANES
            qs = jnp.concatenate(
                [_half_masked_pair(q_ref[r0:r0 + BLOCK, c0:c0 + LANES]),
                 _half_masked_pair(q_ref[r0:r0 + BLOCK, c0 + LANES:c0 + 2 * LANES])], axis=0)
            s = jnp.where(mask, _dot_nt(qs, kcat), NEG_BIG)
            sink = jnp.zeros((group_rows, 1), F32)
            for hh in range(4):
                sink = jnp.where(row_head == hh, sinks_ref[4 * g + hh], sink)
            m = jnp.maximum(jnp.max(s, axis=-1, keepdims=True), sink)
            e = jnp.exp(s - m)
            denom = jnp.sum(e, axis=-1, keepdims=True) + jnp.exp(sink - m)
            probs = (e / denom).astype(BF16)
            o = _dot(probs, vcat)
            mixed_ref[r0:r0 + BLOCK, c0:c0 + LANES] = _merge_halves(o[:2 * BLOCK], BLOCK).astype(BF16)
            mixed_ref[r0:r0 + BLOCK, c0 + LANES:c0 + 2 * LANES] = _merge_halves(o[2 * BLOCK:], BLOCK).astype(BF16)

    _cross_attention(qc_ref, km_ref, vm_ref, mixed_ref, tq)
    o_ref[...] = _out_proj_residual(mixed_ref, wo_ref, g_ref, h_ref)


def _swa_layer(sinks, q, k, v, qc, km, vm, wo_bf16, g, h, batch, seq, tq=512):
    n = h.shape[0]
    mem_len = km.shape[0] // batch
    tiles = seq // tq
    row = lambda b, i, s: (b * tiles + i, 0)
    per_b = lambda b, i, s: (b, 0)
    fixed = lambda b, i, s: (0, 0)
    grid_spec = pltpu.PrefetchScalarGridSpec(
        num_scalar_prefetch=1,
        grid=(batch, tiles),
        in_specs=[
            pl.BlockSpec((tq, SELF_W), row),
            pl.BlockSpec((seq, KV_DUP_W), per_b),
            pl.BlockSpec((seq, KV_DUP_W), per_b),
            pl.BlockSpec((tq, CROSS_W), row),
            pl.BlockSpec((mem_len, CROSS_W), per_b),
            pl.BlockSpec((mem_len, CROSS_W), per_b),
            pl.BlockSpec((D_MODEL, D_MODEL), fixed),
            pl.BlockSpec((1, D_MODEL), fixed),
            pl.BlockSpec((tq, D_MODEL), row),
        ],
        out_specs=pl.BlockSpec((tq, D_MODEL), row),
        scratch_shapes=[pltpu.VMEM((tq, D_MODEL), BF16)],
    )
    return pl.pallas_call(
        functools.partial(_swa_kernel, tq=tq),
        grid_spec=grid_spec,
        out_shape=jax.ShapeDtypeStruct((n, D_MODEL), F32),
        compiler_params=pltpu.CompilerParams(
            dimension_semantics=("arbitrary", "arbitrary"), vmem_limit_bytes=VMEM_LIMIT),
        name="swa_layer",
    )(sinks, q, k, v, qc, km, vm, wo_bf16, g, h)


def _sb_kernel(q_ref, k_ref, v_ref, qc_ref, km_ref, vm_ref, wo_ref, g_ref, h_ref, g2_ref, wr_ref,
               o_ref, xn_ref, route_ref, mixed_ref, qs_ref, acc_ref, carry_ref, *, tq):
    i = pl.program_id(1)
    tk = tq
    rows = 2 * tq
    pairs = SELF_W // LANES

    r_idx = lax.broadcasted_iota(jnp.int32, (rows, tk), 0) % tq
    c_idx = lax.broadcasted_iota(jnp.int32, (rows, tk), 1)
    causal = c_idx < r_idx
    kr = lax.broadcasted_iota(jnp.int32, (tk, tk), 0)
    kc = lax.broadcasted_iota(jnp.int32, (tk, tk), 1)
    suffix = jnp.where(kr >= kc, 1.0, 0.0).astype(BF16)

    for p in range(pairs):
        qs_ref[p] = _half_masked_pair(q_ref[:, p * LANES:(p + 1) * LANES])

    def block(j, diagonal):
        k0 = pl.multiple_of(j * tk, tk)
        for p in range(pairs):
            sl = slice(p * LANES, (p + 1) * LANES)
            z = _dot_nt(qs_ref[p], k_ref[pl.ds(k0, tk), sl])
            sp = jnp.maximum(z, 0.0) + jnp.log(1.0 + jnp.exp2(-jnp.abs(z))) * LOG2E
            if diagonal:
                sp = jnp.where(causal, sp, 0.0)
            s_incl = _dot(sp.astype(BF16), suffix)
            total = jnp.broadcast_to(s_incl[:, 0:1], (rows, LANES))
            if diagonal:
                w = jnp.where(causal, jnp.exp2(z - s_incl), 0.0)
                acc_ref[p] = _dot(w.astype(BF16), v_ref[pl.ds(k0, tk), sl])
                carry_ref[p] = total
            else:
                c = carry_ref[p]
                w = jnp.exp2(z - s_incl - jnp.concatenate([c] * (tk // LANES), axis=1))
                acc_ref[p] += _dot(w.astype(BF16), v_ref[pl.ds(k0, tk), sl])
                carry_ref[p] = c + total

    block(i, True)

    def unfinished(state):
        t, cmin = state
        return (t < i) & (cmin < SB_DONE_LOG2)

    def earlier_block(state):
        t, _ = state
        block(i - 1 - t, False)
        return t + 1, jnp.min(carry_ref[...])

    lax.while_loop(unfinished, earlier_block, (jnp.int32(0), jnp.min(carry_ref[...])))

    for p in range(pairs):
        mixed_ref[:, p * LANES:(p + 1) * LANES] = _merge_halves(acc_ref[p], tq).astype(BF16)

    _cross_attention(qc_ref, km_ref, vm_ref, mixed_ref, tq)
    h1 = _out_proj_residual(mixed_ref, wo_ref, g_ref, h_ref)
    o_ref[...] = h1

    xn = _rms(h1, g2_ref[...])
    xh = xn.astype(BF16)
    xn_ref[...] = xn
    xl = (xn - xh.astype(F32)).astype(BF16)
    wr = wr_ref[...]
    hi = _dot(xh, wr)
    logits = hi[:, :LANES] + hi[:, LANES:] + _dot(xl, wr[:, :LANES])
    lane = _lane_iota(logits.shape)
    l1 = jnp.where(lane < N_EXPERTS, logits, NEG_BIG)
    m1 = jnp.max(l1, axis=-1, keepdims=True)
    i1 = jnp.min(jnp.where(l1 == m1, lane, LANES), axis=-1, keepdims=True)
    l2 = jnp.where(lane == i1, NEG_BIG, l1)
    m2 = jnp.max(l2, axis=-1, keepdims=True)
    i2 = jnp.min(jnp.where(l2 == m2, lane, LANES), axis=-1, keepdims=True)
    e2 = jnp.exp(m2 - m1)
    w1 = 1.0 / (1.0 + e2)
    w2 = e2 / (1.0 + e2)
    lane8 = _lane_iota((tq, 8))
    route_ref[...] = jnp.where(lane8 == 0, i1.astype(F32),
                     jnp.where(lane8 == 1, i2.astype(F32),
                     jnp.where(lane8 == 2, w1, jnp.where(lane8 == 3, w2, 0.0))))


def _sb_layer(q, k, v, qc, km, vm, wo_bf16, g, h, g2, wr_split, batch, seq, tq=256):
    n = h.shape[0]
    mem_len = km.shape[0] // batch
    tiles = seq // tq
    row = lambda b, i: (b * tiles + i, 0)
    per_b = lambda b, i: (b, 0)
    fixed = lambda b, i: (0, 0)
    return pl.pallas_call(
        functools.partial(_sb_kernel, tq=tq),
        grid=(batch, tiles),
        in_specs=[
            pl.BlockSpec((tq, SELF_W), row),
            pl.BlockSpec((seq, SELF_W), per_b),
            pl.BlockSpec((seq, SELF_W), per_b),
            pl.BlockSpec((tq, CROSS_W), row),
            pl.BlockSpec((mem_len, CROSS_W), per_b),
            pl.BlockSpec((mem_len, CROSS_W), per_b),
            pl.BlockSpec((D_MODEL, D_MODEL), fixed),
            pl.BlockSpec((1, D_MODEL), fixed),
            pl.BlockSpec((tq, D_MODEL), row),
            pl.BlockSpec((1, D_MODEL), fixed),
            pl.BlockSpec((D_MODEL, 2 * LANES), fixed),
        ],
        out_specs=[
            pl.BlockSpec((tq, D_MODEL), row),
            pl.BlockSpec((tq, D_MODEL), row),
            pl.BlockSpec((tq, 8), row),
        ],
        out_shape=[
            jax.ShapeDtypeStruct((n, D_MODEL), F32),
            jax.ShapeDtypeStruct((n, D_MODEL), F32),
            jax.ShapeDtypeStruct((n, 8), F32),
        ],
        scratch_shapes=[
            pltpu.VMEM((tq, D_MODEL), BF16),
            pltpu.VMEM((SELF_W // LANES, 2 * tq, LANES), BF16),
            pltpu.VMEM((SELF_W // LANES, 2 * tq, LANES), F32),
            pltpu.VMEM((SELF_W // LANES, 2 * tq, LANES), F32),
        ],
        compiler_params=pltpu.CompilerParams(
            dimension_semantics=("arbitrary", "arbitrary"), vmem_limit_bytes=VMEM_LIMIT),
        name="sb_layer",
    )(q, k, v, qc, km, vm, wo_bf16, g, h, g2, wr_split)


def _swiglu_part(x_bf16, wg_ref, wu_ref, wd_ref):
    gate = _dot(x_bf16, wg_ref[...].astype(BF16))
    up = _dot(x_bf16, wu_ref[...].astype(BF16))
    act = (gate * jax.nn.sigmoid(gate) * up).astype(BF16)
    return _dot(act, wd_ref[...].astype(BF16))


def _dense_ffn_kernel(h_ref, g1_ref, wg_ref, wu_ref, wd_ref, g2_ref, o_ref, xn_ref, acc_ref):
    f = pl.program_id(1)

    @pl.when(f == 0)
    def _():
        xn_ref[...] = _rms(h_ref[...], g1_ref[...]).astype(BF16)

    part = _swiglu_part(xn_ref[...], wg_ref, wu_ref, wd_ref)

    @pl.when(f == 0)
    def _():
        acc_ref[...] = part

    @pl.when(f > 0)
    def _():
        acc_ref[...] += part

    @pl.when(f == pl.num_programs(1) - 1)
    def _():
        o_ref[...] = h_ref[...] + _rms(acc_ref[...], g2_ref[...])


def _dense_ffn(h, g1, wg, wu, wd, g2, tm=1024, tf=512):
    n = h.shape[0]
    row = lambda i, f: (i, 0)
    fixed = lambda i, f: (0, 0)
    return pl.pallas_call(
        _dense_ffn_kernel,
        grid=(n // tm, D_FF // tf),
        in_specs=[
            pl.BlockSpec((tm, D_MODEL), row),
            pl.BlockSpec((1, D_MODEL), fixed),
            pl.BlockSpec((D_MODEL, tf), lambda i, f: (0, f)),
            pl.BlockSpec((D_MODEL, tf), lambda i, f: (0, f)),
            pl.BlockSpec((tf, D_MODEL), lambda i, f: (f, 0)),
            pl.BlockSpec((1, D_MODEL), fixed),
        ],
        out_specs=pl.BlockSpec((tm, D_MODEL), row),
        out_shape=jax.ShapeDtypeStruct((n, D_MODEL), F32),
        scratch_shapes=[pltpu.VMEM((tm, D_MODEL), BF16), pltpu.VMEM((tm, D_MODEL), F32)],
        compiler_params=pltpu.CompilerParams(
            dimension_semantics=("arbitrary", "arbitrary"), vmem_limit_bytes=VMEM_LIMIT),
        name="dense_ffn",
    )(h, g1, wg, wu, wd, g2)


def _moe_ffn_kernel(te_ref, used_ref, xs_ref, wg_ref, wu_ref, wd_ref, o_ref, xb_ref, acc_ref):
    i = pl.program_id(0)
    f = pl.program_id(1)

    @pl.when(i < used_ref[0])
    def _():
        @pl.when(f == 0)
        def _():
            xb_ref[...] = xs_ref[...].astype(BF16)

        part = _swiglu_part(xb_ref[...], wg_ref, wu_ref, wd_ref)

        @pl.when(f == 0)
        def _():
            acc_ref[...] = part

        @pl.when(f > 0)
        def _():
            acc_ref[...] += part

        @pl.when(f == pl.num_programs(1) - 1)
        def _():
            o_ref[...] = acc_ref[...]

    @pl.when((i >= used_ref[0]) & (f == pl.num_programs(1) - 1))
    def _():
        o_ref[...] = jnp.zeros_like(o_ref)


def _moe_ffn(tile_expert, used, xs, wg, wu, wd, tm, tf=512):
    p_rows = xs.shape[0]
    nf = D_FF // tf

    def f_eff(i, f, used_ref):
        return jnp.where(i < used_ref[0], f, nf - 1)

    grid_spec = pltpu.PrefetchScalarGridSpec(
        num_scalar_prefetch=2,
        grid=(p_rows // tm, nf),
        in_specs=[
            pl.BlockSpec((tm, D_MODEL), lambda i, f, te, u: (jnp.minimum(i, u[0] - 1), 0)),
            pl.BlockSpec((None, D_MODEL, tf), lambda i, f, te, u: (te[i], 0, f_eff(i, f, u))),
            pl.BlockSpec((None, D_MODEL, tf), lambda i, f, te, u: (te[i], 0, f_eff(i, f, u))),
            pl.BlockSpec((None, tf, D_MODEL), lambda i, f, te, u: (te[i], f_eff(i, f, u), 0)),
        ],
        out_specs=pl.BlockSpec((tm, D_MODEL), lambda i, f, te, u: (i, 0)),
        scratch_shapes=[pltpu.VMEM((tm, D_MODEL), BF16), pltpu.VMEM((tm, D_MODEL), F32)],
    )
    return pl.pallas_call(
        _moe_ffn_kernel,
        grid_spec=grid_spec,
        out_shape=jax.ShapeDtypeStruct((p_rows, D_MODEL), F32),
        compiler_params=pltpu.CompilerParams(
            dimension_semantics=("arbitrary", "arbitrary"), vmem_limit_bytes=VMEM_LIMIT),
        name="moe_ffn",
    )(tile_expert, used, xs, wg, wu, wd)


def _dispatch_kernel(slot_ref, last_ref, xn_hbm, xs_hbm, zero_ref, sems, *, tt, tm):
    i = pl.program_id(0)
    steps = pl.num_programs(0)

    p_rows = xs_hbm.shape[0]

    def zero_tiles():
        for e in range(N_EXPERTS):
            yield last_ref[e], last_ref[e] >= 0
        for u in range(N_EXPERTS):
            start = last_ref[N_EXPERTS] + u * tm
            yield start, start < p_rows

    def zero_copy(start):
        return pltpu.make_async_copy(zero_ref, xs_hbm.at[pl.ds(pl.multiple_of(start, tm), tm)], sems.at[2])

    @pl.when(i == 0)
    def _():
        zero_ref[...] = jnp.zeros_like(zero_ref)
        for start, exists in zero_tiles():
            @pl.when(exists)
            def _():
                zero_copy(start).start()
        for start, exists in zero_tiles():
            @pl.when(exists)
            def _():
                zero_copy(start).wait()

    def rows_copy(step_sem):
        return pltpu.make_async_copy(xn_hbm.at[pl.ds(0, 2 * tt)], xs_hbm.at[pl.ds(0, 2 * tt)], step_sem)

    sem = sems.at[i % 2]

    def issue(r, carry):
        t = i * tt + r
        src = xn_hbm.at[pl.ds(t, 1)]
        pltpu.make_async_copy(src, xs_hbm.at[pl.ds(slot_ref[2 * t], 1)], sem).start()
        pltpu.make_async_copy(src, xs_hbm.at[pl.ds(slot_ref[2 * t + 1], 1)], sem).start()
        return carry

    lax.fori_loop(0, tt, issue, 0, unroll=8)

    @pl.when(i > 0)
    def _():
        rows_copy(sems.at[(i + 1) % 2]).wait()

    @pl.when(i == steps - 1)
    def _():
        rows_copy(sem).wait()


def _dispatch(slot_flat, last_tile_start, xn, p_rows, tm, tt=512):
    n = xn.shape[0]
    grid_spec = pltpu.PrefetchScalarGridSpec(
        num_scalar_prefetch=2,
        grid=(n // tt,),
        in_specs=[pl.BlockSpec(memory_space=pl.ANY)],
        out_specs=pl.BlockSpec(memory_space=pl.ANY),
        scratch_shapes=[pltpu.VMEM((tm, D_MODEL), F32), pltpu.SemaphoreType.DMA((3,))],
    )
    return pl.pallas_call(
        functools.partial(_dispatch_kernel, tt=tt, tm=tm),
        grid_spec=grid_spec,
        out_shape=jax.ShapeDtypeStruct((p_rows, D_MODEL), F32),
        compiler_params=pltpu.CompilerParams(
            dimension_semantics=("arbitrary",), vmem_limit_bytes=VMEM_LIMIT),
        name="moe_dispatch",
    )(slot_flat, last_tile_start, xn)


def _combine_kernel(slot_ref, h_ref, route_ref, g_ref, ys_hbm, o_ref, ybuf, sems, *, tt):
    i = pl.program_id(0)
    steps = pl.num_programs(0)

    def issue_step(step, buf):
        def issue(r, carry):
            t = step * tt + r
            for k in range(2):
                pltpu.make_async_copy(ys_hbm.at[pl.ds(slot_ref[2 * t + k], 1)],
                                      ybuf.at[buf, k, pl.ds(r, 1)], sems.at[buf]).start()
            return carry
        lax.fori_loop(0, tt, issue, 0, unroll=8)

    @pl.when(i == 0)
    def _():
        issue_step(0, 0)

    @pl.when(i + 1 < steps)
    def _():
        issue_step(i + 1, (i + 1) % 2)

    buf = i % 2
    for k in range(2):
        pltpu.make_async_copy(ys_hbm.at[pl.ds(0, tt)], ybuf.at[buf, k], sems.at[buf]).wait()
    r = route_ref[...]
    y = r[:, 2:3] * ybuf[buf, 0] + r[:, 3:4] * ybuf[buf, 1]
    o_ref[...] = h_ref[...] + _rms(y, g_ref[...])


def _combine(slot_flat, h, route, g, ys, tt=256):
    n = h.shape[0]
    row = lambda i, s: (i, 0)
    grid_spec = pltpu.PrefetchScalarGridSpec(
        num_scalar_prefetch=1,
        grid=(n // tt,),
        in_specs=[
            pl.BlockSpec((tt, D_MODEL), row),
            pl.BlockSpec((tt, 8), row),
            pl.BlockSpec((1, D_MODEL), lambda i, s: (0, 0)),
            pl.BlockSpec(memory_space=pl.ANY),
        ],
        out_specs=pl.BlockSpec((tt, D_MODEL), row),
        scratch_shapes=[pltpu.VMEM((2, 2, tt, D_MODEL), F32), pltpu.SemaphoreType.DMA((2,))],
    )
    return pl.pallas_call(
        functools.partial(_combine_kernel, tt=tt),
        grid_spec=grid_spec,
        out_shape=jax.ShapeDtypeStruct((n, D_MODEL), F32),
        compiler_params=pltpu.CompilerParams(
            dimension_semantics=("arbitrary",), vmem_limit_bytes=VMEM_LIMIT),
        name="moe_combine",
    )(slot_flat, h, route, g, ys)


def _dispatch_plan(route, tm):
    n = route.shape[0]
    blk = 256
    ex = route[:, :2].astype(jnp.int32)
    eids = jnp.arange(N_EXPERTS, dtype=jnp.int32)[None, :]
    member = (ex[:, 0:1] == eids) | (ex[:, 1:2] == eids)
    mb = member.reshape(n // blk, blk, N_EXPERTS).astype(BF16)
    before = (jnp.arange(blk)[:, None] > jnp.arange(blk)[None, :]).astype(BF16)
    local = jnp.einsum('ij,bjk->bik', before, mb, preferred_element_type=F32)
    tot = jnp.sum(mb.astype(F32), axis=1)
    base = jnp.cumsum(tot, axis=0) - tot
    rank = (local + base[:, None, :]).reshape(n, N_EXPERTS).astype(jnp.int32)
    counts = jnp.sum(tot, axis=0).astype(jnp.int32)
    padded = ((counts + tm - 1) // tm) * tm
    ends = jnp.cumsum(padded)
    slot_all = (ends - padded)[None, :] + rank
    slot = jnp.stack([jnp.sum(jnp.where(ex[:, k:k + 1] == eids, slot_all, 0), axis=1) for k in range(2)], axis=1)
    p_rows = 2 * n + N_EXPERTS * tm
    last_tile_start = jnp.concatenate([jnp.where(padded > 0, ends - tm, -1), ends[-1:]]).astype(jnp.int32)
    tile_start = jnp.arange(p_rows // tm, dtype=jnp.int32) * tm
    tile_expert = jnp.sum((tile_start[:, None] >= ends[None, :]).astype(jnp.int32), axis=1)
    used = (ends[-1] // tm).astype(jnp.int32)
    last_expert = jnp.take(tile_expert, jnp.maximum(used - 1, 0))
    tile_expert = jnp.where(tile_start // tm < used, tile_expert, last_expert).astype(jnp.int32)
    return slot.reshape(-1), last_tile_start, tile_expert, used.reshape(1), p_rows


def kernel(x, mem, positions, l0_attn_pre_g, l0_mem_g, l0_w_in, l0_w_mem_kv, l0_sinks, l0_w_out, l0_attn_post_g, l0_ffn_pre_g, l0_w_gate, l0_w_up, l0_w_down, l0_ffn_post_g, l1_attn_pre_g, l1_mem_g, l1_w_in, l1_w_mem_kv, l1_w_out, l1_attn_post_g, l1_ffn_pre_g, l1_w_router, l1_w_gate, l1_w_up, l1_w_down, l1_ffn_post_g):
    batch, seq, _ = x.shape
    n = batch * seq
    h = x.reshape(n, D_MODEL)
    mem2d = mem.reshape(-1, D_MODEL)
    row = lambda g: g.reshape(1, D_MODEL)

    kv0 = SELF_W
    kw = N_KV_HEADS_SWA * HEAD_DIM
    dup = lambda w: jnp.tile(w.reshape(D_MODEL, N_KV_HEADS_SWA, 1, HEAD_DIM), (1, 1, 2, 1)).reshape(D_MODEL, KV_DUP_W)
    w_in0 = jnp.concatenate(
        [l0_w_in[:, :kv0], dup(l0_w_in[:, kv0:kv0 + kw]), dup(l0_w_in[:, kv0 + kw:kv0 + 2 * kw]),
         l0_w_in[:, kv0 + 2 * kw:]], axis=1).astype(BF16)
    inv_freq = ROPE_THETA ** (-jnp.arange(0, HEAD_DIM, 2, dtype=F32) / HEAD_DIM)
    invf = jnp.tile(inv_freq, LANES // (HEAD_DIM // 2)).reshape(1, LANES)

    km0, vm0 = _mem_kv(mem2d, row(l0_mem_g), l0_w_mem_kv.astype(BF16))
    q, k, v, qc = _proj_swa(h, row(l0_attn_pre_g), w_in0, positions.reshape(n, 1), invf)
    h = _swa_layer(l0_sinks, q, k, v, qc, km0, vm0, l0_w_out.astype(BF16), row(l0_attn_post_g), h, batch, seq)
    h = _dense_ffn(h, row(l0_ffn_pre_g), l0_w_gate, l0_w_up, l0_w_down, row(l0_ffn_post_g))

    km1, vm1 = _mem_kv(mem2d, row(l1_mem_g), l1_w_mem_kv.astype(BF16))
    q, k, v, qc = _proj_sb(h, row(l1_attn_pre_g), l1_w_in.astype(BF16))
    wr_pad = jnp.pad(l1_w_router, ((0, 0), (0, LANES - N_EXPERTS)))
    wr_hi = wr_pad.astype(BF16)
    wr_split = jnp.concatenate([wr_hi, (wr_pad - wr_hi.astype(F32)).astype(BF16)], axis=1)
    h, xn, route = _sb_layer(q, k, v, qc, km1, vm1, l1_w_out.astype(BF16), row(l1_attn_post_g), h,
                             row(l1_ffn_pre_g), wr_split, batch, seq)

    tm = 1024
    slot_flat, last_tile_start, tile_expert, used, p_rows = _dispatch_plan(route, tm)
    xs = _dispatch(slot_flat, last_tile_start, xn, p_rows, tm)
    ys = _moe_ffn(tile_expert, used, xs, l1_w_gate, l1_w_up, l1_w_down, tm)
    h = _combine(slot_flat, h, route, row(l1_ffn_post_g), ys)
    return h.reshape(batch, seq, D_MODEL)
```

```python
import functools

import jax
import jax.numpy as jnp
from jax import lax
from jax.experimental import pallas as pl
from jax.experimental.pallas import tpu as pltpu

D_MODEL = 1024
HEAD_DIM = 64
N_SELF_HEADS = 12
N_KV_HEADS_SWA = 3
N_CROSS_HEADS = 4
SWA_WINDOW = 128
BLOCK = 128
ROPE_THETA = 10000.0
D_FF = 3584
N_EXPERTS = 8
RMS_EPS = 1e-6

SELF_W = N_SELF_HEADS * HEAD_DIM
CROSS_W = N_CROSS_HEADS * HEAD_DIM
KV_DUP_W = N_KV_HEADS_SWA * 2 * HEAD_DIM
LANES = 128
Q_SCALE = HEAD_DIM ** -0.5
NEG_BIG = -1e30
LOG2E = 1.4426950408889634
SB_DONE_LOG2 = 160.0

VMEM_LIMIT = 56 * 1024 * 1024

F32 = jnp.float32
BF16 = jnp.bfloat16


def _rms(xf, g):
    var = jnp.mean(xf * xf, axis=-1, keepdims=True)
    return xf * lax.rsqrt(var + RMS_EPS) * g


def _lane_iota(shape):
    return lax.broadcasted_iota(jnp.int32, shape, len(shape) - 1)


def _dot(a, b):
    return jnp.dot(a, b, preferred_element_type=F32)


def _dot_nt(a, b):
    return lax.dot_general(a, b, (((1,), (1,)), ((), ())), preferred_element_type=F32)


def _half_masked_pair(x2):
    lo = _lane_iota(x2.shape) < HEAD_DIM
    zero = jnp.zeros_like(x2)
    return jnp.concatenate([jnp.where(lo, x2, zero), jnp.where(lo, zero, x2)], axis=0)


def _merge_halves(o, r):
    lo = _lane_iota((r, LANES)) < HEAD_DIM
    return jnp.where(lo, o[:r], o[r:])


def _mem_kv_kernel(mem_ref, g_ref, w_ref, k_ref, v_ref):
    xn = _rms(mem_ref[...], g_ref[...]).astype(BF16)
    kv = _dot(xn, w_ref[...])
    k_ref[...] = kv[:, :CROSS_W].astype(BF16)
    v_ref[...] = kv[:, CROSS_W:].astype(BF16)


def _mem_kv(mem2d, g, w_bf16, tm=512):
    n = mem2d.shape[0]
    return pl.pallas_call(
        _mem_kv_kernel,
        grid=(n // tm,),
        in_specs=[
            pl.BlockSpec((tm, D_MODEL), lambda i: (i, 0)),
            pl.BlockSpec((1, D_MODEL), lambda i: (0, 0)),
            pl.BlockSpec((D_MODEL, 2 * CROSS_W), lambda i: (0, 0)),
        ],
        out_specs=[
            pl.BlockSpec((tm, CROSS_W), lambda i: (i, 0)),
            pl.BlockSpec((tm, CROSS_W), lambda i: (i, 0)),
        ],
        out_shape=[jax.ShapeDtypeStruct((n, CROSS_W), BF16)] * 2,
        compiler_params=pltpu.CompilerParams(
            dimension_semantics=("arbitrary",), vmem_limit_bytes=VMEM_LIMIT),
        name="mem_kv",
    )(mem2d, g, w_bf16)


def _rope_chunk(xc, cos, sin, first_half):
    rot = jnp.where(first_half, -pltpu.roll(xc, LANES - HEAD_DIM // 2, axis=1),
                    pltpu.roll(xc, HEAD_DIM // 2, axis=1))
    return xc * cos + rot * sin


def _proj_swa_kernel(h_ref, g_ref, w_ref, pos_ref, invf_ref, q_ref, k_ref, v_ref, qc_ref):
    xn = _rms(h_ref[...], g_ref[...]).astype(BF16)
    proj = _dot(xn, w_ref[...])
    ang = pos_ref[...].astype(F32) * invf_ref[...]
    cos = jnp.cos(ang)
    sin = jnp.sin(ang)
    first_half = (_lane_iota(cos.shape) % HEAD_DIM) < (HEAD_DIM // 2)
    for c in range(SELF_W // LANES):
        xc = proj[:, c * LANES:(c + 1) * LANES]
        q_ref[:, c * LANES:(c + 1) * LANES] = (
            _rope_chunk(xc, cos, sin, first_half) * Q_SCALE).astype(BF16)
    for c in range(KV_DUP_W // LANES):
        xc = proj[:, SELF_W + c * LANES:SELF_W + (c + 1) * LANES]
        k_ref[:, c * LANES:(c + 1) * LANES] = _rope_chunk(xc, cos, sin, first_half).astype(BF16)
    v0 = SELF_W + KV_DUP_W
    v_ref[...] = proj[:, v0:v0 + KV_DUP_W].astype(BF16)
    qc_ref[...] = (proj[:, v0 + KV_DUP_W:] * Q_SCALE).astype(BF16)


def _proj_swa(h, g, w_bf16, pos, invf, tm=512):
    n = h.shape[0]
    w_cols = w_bf16.shape[1]
    row = lambda i: (i, 0)
    fixed = lambda i: (0, 0)
    return pl.pallas_call(
        _proj_swa_kernel,
        grid=(n // tm,),
        in_specs=[
            pl.BlockSpec((tm, D_MODEL), row),
            pl.BlockSpec((1, D_MODEL), fixed),
            pl.BlockSpec((D_MODEL, w_cols), fixed),
            pl.BlockSpec((tm, 1), row),
            pl.BlockSpec((1, LANES), fixed),
        ],
        out_specs=[
            pl.BlockSpec((tm, SELF_W), row),
            pl.BlockSpec((tm, KV_DUP_W), row),
            pl.BlockSpec((tm, KV_DUP_W), row),
            pl.BlockSpec((tm, CROSS_W), row),
        ],
        out_shape=[
            jax.ShapeDtypeStruct((n, SELF_W), BF16),
            jax.ShapeDtypeStruct((n, KV_DUP_W), BF16),
            jax.ShapeDtypeStruct((n, KV_DUP_W), BF16),
            jax.ShapeDtypeStruct((n, CROSS_W), BF16),
        ],
        compiler_params=pltpu.CompilerParams(
            dimension_semantics=("arbitrary",), vmem_limit_bytes=VMEM_LIMIT),
        name="proj_swa",
    )(h, g, w_bf16, pos, invf)


def _proj_sb_kernel(h_ref, g_ref, w_ref, q_ref, k_ref, v_ref, qc_ref):
    xn = _rms(h_ref[...], g_ref[...]).astype(BF16)
    proj = _dot(xn, w_ref[...])
    q_ref[...] = (proj[:, :SELF_W] * (Q_SCALE * LOG2E)).astype(BF16)
    k_ref[...] = proj[:, SELF_W:2 * SELF_W].astype(BF16)
    v_ref[...] = proj[:, 2 * SELF_W:3 * SELF_W].astype(BF16)
    qc_ref[...] = (proj[:, 3 * SELF_W:] * Q_SCALE).astype(BF16)


def _proj_sb(h, g, w_bf16, tm=512):
    n = h.shape[0]
    row = lambda i: (i, 0)
    fixed = lambda i: (0, 0)
    return pl.pallas_call(
        _proj_sb_kernel,
        grid=(n // tm,),
        in_specs=[
            pl.BlockSpec((tm, D_MODEL), row),
            pl.BlockSpec((1, D_MODEL), fixed),
            pl.BlockSpec((D_MODEL, w_bf16.shape[1]), fixed),
        ],
        out_specs=[
            pl.BlockSpec((tm, SELF_W), row),
            pl.BlockSpec((tm, SELF_W), row),
            pl.BlockSpec((tm, SELF_W), row),
            pl.BlockSpec((tm, CROSS_W), row),
        ],
        out_shape=[
            jax.ShapeDtypeStruct((n, SELF_W), BF16),
            jax.ShapeDtypeStruct((n, SELF_W), BF16),
            jax.ShapeDtypeStruct((n, SELF_W), BF16),
            jax.ShapeDtypeStruct((n, CROSS_W), BF16),
        ],
        compiler_params=pltpu.CompilerParams(
            dimension_semantics=("arbitrary",), vmem_limit_bytes=VMEM_LIMIT),
        name="proj_sb",
    )(h, g, w_bf16)


def _cross_attention(qc_ref, km_ref, vm_ref, mixed_ref, tq):
    for p in range(CROSS_W // LANES):
        sl = slice(p * LANES, (p + 1) * LANES)
        qs = _half_masked_pair(qc_ref[:, sl])
        s = _dot_nt(qs, km_ref[:, sl])
        m = jnp.max(s, axis=-1, keepdims=True)
        e = jnp.exp(s - m)
        probs = (e / jnp.sum(e, axis=-1, keepdims=True)).astype(BF16)
        o = _dot(probs, vm_ref[:, sl])
        mixed_ref[:, SELF_W + p * LANES:SELF_W + (p + 1) * LANES] = _merge_halves(o, tq).astype(BF16)


def _out_proj_residual(mixed_ref, wo_ref, g_ref, h_ref):
    mixed = _dot(mixed_ref[...], wo_ref[...])
    return h_ref[...] + _rms(mixed, g_ref[...])


def _swa_kernel(sinks_ref, q_ref, k_ref, v_ref, qc_ref, km_ref, vm_ref, wo_ref, g_ref, h_ref,
                o_ref, mixed_ref, *, tq):
    i = pl.program_id(1)
    blocks = tq // BLOCK
    group_rows = 4 * BLOCK

    qi = lax.broadcasted_iota(jnp.int32, (group_rows, 2 * BLOCK), 0) % BLOCK
    kj = lax.broadcasted_iota(jnp.int32, (group_rows, 2 * BLOCK), 1)
    diff = qi + BLOCK - kj
    in_win = (diff >= 0) & (diff < SWA_WINDOW)
    row_head = lax.broadcasted_iota(jnp.int32, (group_rows, 1), 0) // BLOCK

    for blk in range(blocks):
        n = i * blocks + blk
        r0 = blk * BLOCK
        cur0 = pl.multiple_of(n * BLOCK, BLOCK)
        prev0 = pl.multiple_of(jnp.maximum(n - 1, 0) * BLOCK, BLOCK)
        mask = in_win & ((kj >= BLOCK) | (n > 0))
        for g in range(N_KV_HEADS_SWA):
            gs = slice(g * LANES, (g + 1) * LANES)
            kcat = jnp.concatenate([k_ref[pl.ds(prev0, BLOCK), gs], k_ref[pl.ds(cur0, BLOCK), gs]], axis=0)
            vcat = jnp.concatenate([v_ref[pl.ds(prev0, BLOCK), gs], v_ref[pl.ds(cur0, BLOCK), gs]], axis=0)
            c0 = g * 2 * LANES
            qs = jnp.concatenate(
                [_half_masked_pair(q_ref[r0:r0 + BLOCK, c0:c0 + LANES]),
                 _half_masked_pair(q_ref[r0:r0 + BLOCK, c0 + LANES:c0 + 2 * LANES])], axis=0)
            s = jnp.where(mask, _dot_nt(qs, kcat), NEG_BIG)
            sink = jnp.zeros((group_rows, 1), F32)
            for hh in range(4):
                sink = jnp.where(row_head == hh, sinks_ref[4 * g + hh], sink)
            m = jnp.maximum(jnp.max(s, axis=-1, keepdims=True), sink)
            e = jnp.exp(s - m)
            denom = jnp.sum(e, axis=-1, keepdims=True) + jnp.exp(sink - m)
            probs = (e / denom).astype(BF16)
            o = _dot(probs, vcat)
            mixed_ref[r0:r0 + BLOCK, c0:c0 + LANES] = _merge_halves(o[:2 * BLOCK], BLOCK).astype(BF16)
            mixed_ref[r0:r0 + BLOCK, c0 + LANES:c0 + 2 * LANES] = _merge_halves(o[2 * BLOCK:], BLOCK).astype(BF16)

    _cross_attention(qc_ref, km_ref, vm_ref, mixed_ref, tq)
    o_ref[...] = _out_proj_residual(mixed_ref, wo_ref, g_ref, h_ref)


def _swa_layer(sinks, q, k, v, qc, km, vm, wo_bf16, g, h, batch, seq, tq=512):
    n = h.shape[0]
    mem_len = km.shape[0] // batch
    tiles = seq // tq
    row = lambda b, i, s: (b * tiles + i, 0)
    per_b = lambda b, i, s: (b, 0)
    fixed = lambda b, i, s: (0, 0)
    grid_spec = pltpu.PrefetchScalarGridSpec(
        num_scalar_prefetch=1,
        grid=(batch, tiles),
        in_specs=[
            pl.BlockSpec((tq, SELF_W), row),
            pl.BlockSpec((seq, KV_DUP_W), per_b),
            pl.BlockSpec((seq, KV_DUP_W), per_b),
            pl.BlockSpec((tq, CROSS_W), row),
            pl.BlockSpec((mem_len, CROSS_W), per_b),
            pl.BlockSpec((mem_len, CROSS_W), per_b),
            pl.BlockSpec((D_MODEL, D_MODEL), fixed),
            pl.BlockSpec((1, D_MODEL), fixed),
            pl.BlockSpec((tq, D_MODEL), row),
        ],
        out_specs=pl.BlockSpec((tq, D_MODEL), row),
        scratch_shapes=[pltpu.VMEM((tq, D_MODEL), BF16)],
    )
    return pl.pallas_call(
        functools.partial(_swa_kernel, tq=tq),
        grid_spec=grid_spec,
        out_shape=jax.ShapeDtypeStruct((n, D_MODEL), F32),
        compiler_params=pltpu.CompilerParams(
            dimension_semantics=("arbitrary", "arbitrary"), vmem_limit_bytes=VMEM_LIMIT),
        name="swa_layer",
    )(sinks, q, k, v, qc, km, vm, wo_bf16, g, h)


def _sb_kernel(q_ref, k_ref, v_ref, qc_ref, km_ref, vm_ref, wo_ref, g_ref, h_ref, g2_ref, wr_ref,
               o_ref, xn_ref, route_ref, mixed_ref, qs_ref, acc_ref, carry_ref, *, tq):
    i = pl.program_id(1)
    tk = tq
    rows = 2 * tq
    pairs = SELF_W // LANES

    r_idx = lax.broadcasted_iota(jnp.int32, (rows, tk), 0) % tq
    c_idx = lax.broadcasted_iota(jnp.int32, (rows, tk), 1)
    causal = c_idx < r_idx
    kr = lax.broadcasted_iota(jnp.int32, (tk, tk), 0)
    kc = lax.broadcasted_iota(jnp.int32, (tk, tk), 1)
    suffix = jnp.where(kr >= kc, 1.0, 0.0).astype(BF16)

    for p in range(pairs):
        qs_ref[p] = _half_masked_pair(q_ref[:, p * LANES:(p + 1) * LANES])

    def block(j, diagonal):
        k0 = pl.multiple_of(j * tk, tk)
        for p in range(pairs):
            sl = slice(p * LANES, (p + 1) * LANES)
            z = _dot_nt(qs_ref[p], k_ref[pl.ds(k0, tk), sl])
            sp = jnp.maximum(z, 0.0) + jnp.log(1.0 + jnp.exp2(-jnp.abs(z))) * LOG2E
            if diagonal:
                sp = jnp.where(causal, sp, 0.0)
            s_incl = _dot(sp.astype(BF16), suffix)
            total = jnp.broadcast_to(s_incl[:, 0:1], (rows, LANES))
            if diagonal:
                w = jnp.where(causal, jnp.exp2(z - s_incl), 0.0)
                acc_ref[p] = _dot(w.astype(BF16), v_ref[pl.ds(k0, tk), sl])
                carry_ref[p] = total
            else:
                c = carry_ref[p]
                w = jnp.exp2(z - s_incl - jnp.concatenate([c] * (tk // LANES), axis=1))
                acc_ref[p] += _dot(w.astype(BF16), v_ref[pl.ds(k0, tk), sl])
                carry_ref[p] = c + total

    block(i, True)

    def unfinished(state):
        t, cmin = state
        return (t < i) & (cmin < SB_DONE_LOG2)

    def earlier_block(state):
        t, _ = state
        block(i - 1 - t, False)
        return t + 1, jnp.min(carry_ref[...])

    lax.while_loop(unfinished, earlier_block, (jnp.int32(0), jnp.min(carry_ref[...])))

    for p in range(pairs):
        mixed_ref[:, p * LANES:(p + 1) * LANES] = _merge_halves(acc_ref[p], tq).astype(BF16)

    _cross_attention(qc_ref, km_ref, vm_ref, mixed_ref, tq)
    h1 = _out_proj_residual(mixed_ref, wo_ref, g_ref, h_ref)
    o_ref[...] = h1

    xn = _rms(h1, g2_ref[...])
    xh = xn.astype(BF16)
    xn_ref[...] = xn
    xl = (xn - xh.astype(F32)).astype(BF16)
    wr = wr_ref[...]
    hi = _dot(xh, wr)
    logits = hi[:, :LANES] + hi[:, LANES:] + _dot(xl, wr[:, :LANES])
    lane = _lane_iota(logits.shape)
    l1 = jnp.where(lane < N_EXPERTS, logits, NEG_BIG)
    m1 = jnp.max(l1, axis=-1, keepdims=True)
    i1 = jnp.min(jnp.where(l1 == m1, lane, LANES), axis=-1, keepdims=True)
    l2 = jnp.where(lane == i1, NEG_BIG, l1)
    m2 = jnp.max(l2, axis=-1, keepdims=True)
    i2 = jnp.min(jnp.where(l2 == m2, lane, LANES), axis=-1, keepdims=True)
    e2 = jnp.exp(m2 - m1)
    w1 = 1.0 / (1.0 + e2)
    w2 = e2 / (1.0 + e2)
    lane8 = _lane_iota((tq, 8))
    route_ref[...] = jnp.where(lane8 == 0, i1.astype(F32),
                     jnp.where(lane8 == 1, i2.astype(F32),
                     jnp.where(lane8 == 2, w1, jnp.where(lane8 == 3, w2, 0.0))))


def _sb_layer(q, k, v, qc, km, vm, wo_bf16, g, h, g2, wr_split, batch, seq, tq=256):
    n = h.shape[0]
    mem_len = km.shape[0] // batch
    tiles = seq // tq
    row = lambda b, i: (b * tiles + i, 0)
    per_b = lambda b, i: (b, 0)
    fixed = lambda b, i: (0, 0)
    return pl.pallas_call(
        functools.partial(_sb_kernel, tq=tq),
        grid=(batch, tiles),
        in_specs=[
            pl.BlockSpec((tq, SELF_W), row),
            pl.BlockSpec((seq, SELF_W), per_b),
            pl.BlockSpec((seq, SELF_W), per_b),
            pl.BlockSpec((tq, CROSS_W), row),
            pl.BlockSpec((mem_len, CROSS_W), per_b),
            pl.BlockSpec((mem_len, CROSS_W), per_b),
            pl.BlockSpec((D_MODEL, D_MODEL), fixed),
            pl.BlockSpec((1, D_MODEL), fixed),
            pl.BlockSpec((tq, D_MODEL), row),
            pl.BlockSpec((1, D_MODEL), fixed),
            pl.BlockSpec((D_MODEL, 2 * LANES), fixed),
        ],
        out_specs=[
            pl.BlockSpec((tq, D_MODEL), row),
            pl.BlockSpec((tq, D_MODEL), row),
            pl.BlockSpec((tq, 8), row),
        ],
        out_shape=[
            jax.ShapeDtypeStruct((n, D_MODEL), F32),
            jax.ShapeDtypeStruct((n, D_MODEL), F32),
            jax.ShapeDtypeStruct((n, 8), F32),
        ],
        scratch_shapes=[
            pltpu.VMEM((tq, D_MODEL), BF16),
            pltpu.VMEM((SELF_W // LANES, 2 * tq, LANES), BF16),
            pltpu.VMEM((SELF_W // LANES, 2 * tq, LANES), F32),
            pltpu.VMEM((SELF_W // LANES, 2 * tq, LANES), F32),
        ],
        compiler_params=pltpu.CompilerParams(
            dimension_semantics=("arbitrary", "arbitrary"), vmem_limit_bytes=VMEM_LIMIT),
        name="sb_layer",
    )(q, k, v, qc, km, vm, wo_bf16, g, h, g2, wr_split)


def _swiglu_part(x_bf16, wg_ref, wu_ref, wd_ref):
    gate = _dot(x_bf16, wg_ref[...].astype(BF16))
    up = _dot(x_bf16, wu_ref[...].astype(BF16))
    act = (gate * jax.nn.sigmoid(gate) * up).astype(BF16)
    return _dot(act, wd_ref[...].astype(BF16))


def _dense_ffn_kernel(h_ref, g1_ref, wg_ref, wu_ref, wd_ref, g2_ref, o_ref, xn_ref):
    f = pl.program_id(1)

    @pl.when(f == 0)
    def _():
        xn_ref[...] = _rms(h_ref[...], g1_ref[...]).astype(BF16)
        o_ref[...] = jnp.zeros_like(o_ref)

    o_ref[...] += _swiglu_part(xn_ref[...], wg_ref, wu_ref, wd_ref)

    @pl.when(f == pl.num_programs(1) - 1)
    def _():
        o_ref[...] = h_ref[...] + _rms(o_ref[...], g2_ref[...])


def _dense_ffn(h, g1, wg, wu, wd, g2, tm=1024, tf=512):
    n = h.shape[0]
    row = lambda i, f: (i, 0)
    fixed = lambda i, f: (0, 0)
    return pl.pallas_call(
        _dense_ffn_kernel,
        grid=(n // tm, D_FF // tf),
        in_specs=[
            pl.BlockSpec((tm, D_MODEL), row),
            pl.BlockSpec((1, D_MODEL), fixed),
            pl.BlockSpec((D_MODEL, tf), lambda i, f: (0, f)),
            pl.BlockSpec((D_MODEL, tf), lambda i, f: (0, f)),
            pl.BlockSpec((tf, D_MODEL), lambda i, f: (f, 0)),
            pl.BlockSpec((1, D_MODEL), fixed),
        ],
        out_specs=pl.BlockSpec((tm, D_MODEL), row),
        out_shape=jax.ShapeDtypeStruct((n, D_MODEL), F32),
        scratch_shapes=[pltpu.VMEM((tm, D_MODEL), BF16)],
        compiler_params=pltpu.CompilerParams(
            dimension_semantics=("arbitrary", "arbitrary"), vmem_limit_bytes=VMEM_LIMIT),
        name="dense_ffn",
    )(h, g1, wg, wu, wd, g2)


def _moe_ffn_kernel(te_ref, used_ref, xs_ref, wg_ref, wu_ref, wd_ref, o_ref, xb_ref):
    i = pl.program_id(0)
    f = pl.program_id(1)

    @pl.when(i < used_ref[0])
    def _():
        @pl.when(f == 0)
        def _():
            xb_ref[...] = xs_ref[...].astype(BF16)
            o_ref[...] = jnp.zeros_like(o_ref)

        o_ref[...] += _swiglu_part(xb_ref[...], wg_ref, wu_ref, wd_ref)

    @pl.when((i >= used_ref[0]) & (f == pl.num_programs(1) - 1))
    def _():
        o_ref[...] = jnp.zeros_like(o_ref)


def _moe_ffn(tile_expert, used, xs, wg, wu, wd, tm, tf=512):
    p_rows = xs.shape[0]
    nf = D_FF // tf

    def f_eff(i, f, used_ref):
        return jnp.where(i < used_ref[0], f, nf - 1)

    grid_spec = pltpu.PrefetchScalarGridSpec(
        num_scalar_prefetch=2,
        grid=(p_rows // tm, nf),
        in_specs=[
            pl.BlockSpec((tm, D_MODEL), lambda i, f, te, u: (jnp.minimum(i, u[0] - 1), 0)),
            pl.BlockSpec((None, D_MODEL, tf), lambda i, f, te, u: (te[i], 0, f_eff(i, f, u))),
            pl.BlockSpec((None, D_MODEL, tf), lambda i, f, te, u: (te[i], 0, f_eff(i, f, u))),
            pl.BlockSpec((None, tf, D_MODEL), lambda i, f, te, u: (te[i], f_eff(i, f, u), 0)),
        ],
        out_specs=pl.BlockSpec((tm, D_MODEL), lambda i, f, te, u: (i, 0)),
        scratch_shapes=[pltpu.VMEM((tm, D_MODEL), BF16)],
    )
    return pl.pallas_call(
        _moe_ffn_kernel,
        grid_spec=grid_spec,
        out_shape=jax.ShapeDtypeStruct((p_rows, D_MODEL), F32),
        compiler_params=pltpu.CompilerParams(
            dimension_semantics=("arbitrary", "arbitrary"), vmem_limit_bytes=VMEM_LIMIT),
        name="moe_ffn",
    )(tile_expert, used, xs, wg, wu, wd)


def _dispatch_kernel(slot_ref, last_ref, xn_ref, xs_hbm, zero_ref, sems, *, tt, tm):
    i = pl.program_id(0)
    p_rows = xs_hbm.shape[0]

    def zero_tiles():
        for e in range(N_EXPERTS):
            yield last_ref[e], last_ref[e] >= 0
        for u in range(N_EXPERTS):
            start = last_ref[N_EXPERTS] + u * tm
            yield start, start < p_rows

    def zero_copy(start):
        return pltpu.make_async_copy(zero_ref, xs_hbm.at[pl.ds(pl.multiple_of(start, tm), tm)], sems.at[1])

    @pl.when(i == 0)
    def _():
        zero_ref[...] = jnp.zeros_like(zero_ref)
        for start, exists in zero_tiles():
            @pl.when(exists)
            def _():
                zero_copy(start).start()
        for start, exists in zero_tiles():
            @pl.when(exists)
            def _():
                zero_copy(start).wait()

    sem = sems.at[0]

    def issue(r, carry):
        t = i * tt + r
        src = xn_ref.at[pl.ds(r, 1)]
        pltpu.make_async_copy(src, xs_hbm.at[pl.ds(slot_ref[2 * t], 1)], sem).start()
        pltpu.make_async_copy(src, xs_hbm.at[pl.ds(slot_ref[2 * t + 1], 1)], sem).start()
        return carry

    lax.fori_loop(0, tt, issue, 0, unroll=8)

    for _ in range(2):
        pltpu.make_async_copy(xn_ref, xs_hbm.at[pl.ds(0, tt)], sem).wait()


def _dispatch(slot_flat, last_tile_start, xn, p_rows, tm, tt=1024):
    n = xn.shape[0]
    grid_spec = pltpu.PrefetchScalarGridSpec(
        num_scalar_prefetch=2,
        grid=(n // tt,),
        in_specs=[pl.BlockSpec((tt, D_MODEL), lambda i, s, l: (i, 0))],
        out_specs=pl.BlockSpec(memory_space=pl.ANY),
        scratch_shapes=[pltpu.VMEM((tm, D_MODEL), F32), pltpu.SemaphoreType.DMA((2,))],
    )
    return pl.pallas_call(
        functools.partial(_dispatch_kernel, tt=tt, tm=tm),
        grid_spec=grid_spec,
        out_shape=jax.ShapeDtypeStruct((p_rows, D_MODEL), F32),
        compiler_params=pltpu.CompilerParams(
            dimension_semantics=("arbitrary",), vmem_limit_bytes=VMEM_LIMIT),
        name="moe_dispatch",
    )(slot_flat, last_tile_start, xn)


def _combine_kernel(slot_ref, h_ref, route_ref, g_ref, ys_hbm, o_ref, ybuf, sems, *, tt):
    i = pl.program_id(0)
    steps = pl.num_programs(0)

    def issue_step(step, buf):
        def issue(r, carry):
            t = step * tt + r
            for k in range(2):
                pltpu.make_async_copy(ys_hbm.at[pl.ds(slot_ref[2 * t + k], 1)],
                                      ybuf.at[buf, k, pl.ds(r, 1)], sems.at[buf]).start()
            return carry
        lax.fori_loop(0, tt, issue, 0, unroll=8)

    @pl.when(i == 0)
    def _():
        issue_step(0, 0)

    @pl.when(i + 1 < steps)
    def _():
        issue_step(i + 1, (i + 1) % 2)

    buf = i % 2
    for k in range(2):
        pltpu.make_async_copy(ys_hbm.at[pl.ds(0, tt)], ybuf.at[buf, k], sems.at[buf]).wait()
    r = route_ref[...]
    y = r[:, 2:3] * ybuf[buf, 0] + r[:, 3:4] * ybuf[buf, 1]
    o_ref[...] = h_ref[...] + _rms(y, g_ref[...])


def _combine(slot_flat, h, route, g, ys, tt=256):
    n = h.shape[0]
    row = lambda i, s: (i, 0)
    grid_spec = pltpu.PrefetchScalarGridSpec(
        num_scalar_prefetch=1,
        grid=(n // tt,),
        in_specs=[
            pl.BlockSpec((tt, D_MODEL), row),
            pl.BlockSpec((tt, 8), row),
            pl.BlockSpec((1, D_MODEL), lambda i, s: (0, 0)),
            pl.BlockSpec(memory_space=pl.ANY),
        ],
        out_specs=pl.BlockSpec((tt, D_MODEL), row),
        scratch_shapes=[pltpu.VMEM((2, 2, tt, D_MODEL), F32), pltpu.SemaphoreType.DMA((2,))],
    )
    return pl.pallas_call(
        functools.partial(_combine_kernel, tt=tt),
        grid_spec=grid_spec,
        out_shape=jax.ShapeDtypeStruct((n, D_MODEL), F32),
        compiler_params=pltpu.CompilerParams(
            dimension_semantics=("arbitrary",), vmem_limit_bytes=VMEM_LIMIT),
        name="moe_combine",
    )(slot_flat, h, route, g, ys)


def _dispatch_plan(route, tm):
    n = route.shape[0]
    blk = 256
    ex = route[:, :2].astype(jnp.int32)
    eids = jnp.arange(N_EXPERTS, dtype=jnp.int32)[None, :]
    member = (ex[:, 0:1] == eids) | (ex[:, 1:2] == eids)
    mb = member.reshape(n // blk, blk, N_EXPERTS).astype(BF16)
    before = (jnp.arange(blk)[:, None] > jnp.arange(blk)[None, :]).astype(BF16)
    local = jnp.einsum('ij,bjk->bik', before, mb, preferred_element_type=F32)
    tot = jnp.sum(mb.astype(F32), axis=1)
    base = jnp.cumsum(tot, axis=0) - tot
    rank = (local + base[:, None, :]).reshape(n, N_EXPERTS).astype(jnp.int32)
    counts = jnp.sum(tot, axis=0).astype(jnp.int32)
    padded = ((counts + tm - 1) // tm) * tm
    ends = jnp.cumsum(padded)
    slot_all = (ends - padded)[None, :] + rank
    slot = jnp.stack([jnp.sum(jnp.where(ex[:, k:k + 1] == eids, slot_all, 0), axis=1) for k in range(2)], axis=1)
    p_rows = 2 * n + N_EXPERTS * tm
    last_tile_start = jnp.concatenate([jnp.where(padded > 0, ends - tm, -1), ends[-1:]]).astype(jnp.int32)
    tile_start = jnp.arange(p_rows // tm, dtype=jnp.int32) * tm
    tile_expert = jnp.sum((tile_start[:, None] >= ends[None, :]).astype(jnp.int32), axis=1)
    used = (ends[-1] // tm).astype(jnp.int32)
    last_expert = jnp.take(tile_expert, jnp.maximum(used - 1, 0))
    tile_expert = jnp.where(tile_start // tm < used, tile_expert, last_expert).astype(jnp.int32)
    return slot.reshape(-1), last_tile_start, tile_expert, used.reshape(1), p_rows


def kernel(x, mem, positions, l0_attn_pre_g, l0_mem_g, l0_w_in, l0_w_mem_kv, l0_sinks, l0_w_out, l0_attn_post_g, l0_ffn_pre_g, l0_w_gate, l0_w_up, l0_w_down, l0_ffn_post_g, l1_attn_pre_g, l1_mem_g, l1_w_in, l1_w_mem_kv, l1_w_out, l1_attn_post_g, l1_ffn_pre_g, l1_w_router, l1_w_gate, l1_w_up, l1_w_down, l1_ffn_post_g):
    batch, seq, _ = x.shape
    n = batch * seq
    h = x.reshape(n, D_MODEL)
    mem2d = mem.reshape(-1, D_MODEL)
    row = lambda g: g.reshape(1, D_MODEL)

    kv0 = SELF_W
    kw = N_KV_HEADS_SWA * HEAD_DIM
    dup = lambda w: jnp.tile(w.reshape(D_MODEL, N_KV_HEADS_SWA, 1, HEAD_DIM), (1, 1, 2, 1)).reshape(D_MODEL, KV_DUP_W)
    w_in0 = jnp.concatenate(
        [l0_w_in[:, :kv0], dup(l0_w_in[:, kv0:kv0 + kw]), dup(l0_w_in[:, kv0 + kw:kv0 + 2 * kw]),
         l0_w_in[:, kv0 + 2 * kw:]], axis=1).astype(BF16)
    inv_freq = ROPE_THETA ** (-jnp.arange(0, HEAD_DIM, 2, dtype=F32) / HEAD_DIM)
    invf = jnp.tile(inv_freq, LANES // (HEAD_DIM // 2)).reshape(1, LANES)

    km0, vm0 = _mem_kv(mem2d, row(l0_mem_g), l0_w_mem_kv.astype(BF16))
    q, k, v, qc = _proj_swa(h, row(l0_attn_pre_g), w_in0, positions.reshape(n, 1), invf)
    h = _swa_layer(l0_sinks, q, k, v, qc, km0, vm0, l0_w_out.astype(BF16), row(l0_attn_post_g), h, batch, seq)
    h = _dense_ffn(h, row(l0_ffn_pre_g), l0_w_gate, l0_w_up, l0_w_down, row(l0_ffn_post_g))

    km1, vm1 = _mem_kv(mem2d, row(l1_mem_g), l1_w_mem_kv.astype(BF16))
    q, k, v, qc = _proj_sb(h, row(l1_attn_pre_g), l1_w_in.astype(BF16))
    wr_pad = jnp.pad(l1_w_router, ((0, 0), (0, LANES - N_EXPERTS)))
    wr_hi = wr_pad.astype(BF16)
    wr_split = jnp.concatenate([wr_hi, (wr_pad - wr_hi.astype(F32)).astype(BF16)], axis=1)
    h, xn, route = _sb_layer(q, k, v, qc, km1, vm1, l1_w_out.astype(BF16), row(l1_attn_post_g), h,
                             row(l1_ffn_pre_g), wr_split, batch, seq)

    tm = 1024
    slot_flat, last_tile_start, tile_expert, used, p_rows = _dispatch_plan(route, tm)
    xs = _dispatch(slot_flat, last_tile_start, xn, p_rows, tm)
    ys = _moe_ffn(tile_expert, used, xs, l1_w_gate, l1_w_up, l1_w_down, tm)
    h = _combine(slot_flat, h, route, row(l1_ffn_post_g), ys)
    return h.reshape(batch, seq, D_MODEL)
```

```python
import functools

import jax
import jax.numpy as jnp
from jax import lax
from jax.experimental import pallas as pl
from jax.experimental.pallas import tpu as pltpu

D_MODEL = 1024
HEAD_DIM = 64
N_SELF_HEADS = 12
N_KV_HEADS_SWA = 3
N_CROSS_HEADS = 4
SWA_WINDOW = 128
BLOCK = 128
ROPE_THETA = 10000.0
D_FF = 3584
N_EXPERTS = 8
RMS_EPS = 1e-6

SELF_W = N_SELF_HEADS * HEAD_DIM
CROSS_W = N_CROSS_HEADS * HEAD_DIM
KV_DUP_W = N_KV_HEADS_SWA * 2 * HEAD_DIM
LANES = 128
Q_SCALE = HEAD_DIM ** -0.5
NEG_BIG = -1e30
LOG2E = 1.4426950408889634
SB_DONE_LOG2 = 160.0

VMEM_LIMIT = 56 * 1024 * 1024

F32 = jnp.float32
BF16 = jnp.bfloat16


def _rms(xf, g):
    var = jnp.mean(xf * xf, axis=-1, keepdims=True)
    return xf * lax.rsqrt(var + RMS_EPS) * g


def _lane_iota(shape):
    return lax.broadcasted_iota(jnp.int32, shape, len(shape) - 1)


def _dot(a, b):
    return jnp.dot(a, b, preferred_element_type=F32)


def _dot_nt(a, b):
    return lax.dot_general(a, b, (((1,), (1,)), ((), ())), preferred_element_type=F32)


def _half_masked_pair(x2):
    lo = _lane_iota(x2.shape) < HEAD_DIM
    zero = jnp.zeros_like(x2)
    return jnp.concatenate([jnp.where(lo, x2, zero), jnp.where(lo, zero, x2)], axis=0)


def _merge_halves(o, r):
    lo = _lane_iota((r, LANES)) < HEAD_DIM
    return jnp.where(lo, o[:r], o[r:])


def _mem_kv_kernel(mem_ref, g_ref, w_ref, k_ref, v_ref):
    xn = _rms(mem_ref[...], g_ref[...]).astype(BF16)
    kv = _dot(xn, w_ref[...])
    k_ref[...] = kv[:, :CROSS_W].astype(BF16)
    v_ref[...] = kv[:, CROSS_W:].astype(BF16)


def _mem_kv(mem2d, g, w_bf16, tm=512):
    n = mem2d.shape[0]
    return pl.pallas_call(
        _mem_kv_kernel,
        grid=(n // tm,),
        in_specs=[
            pl.BlockSpec((tm, D_MODEL), lambda i: (i, 0)),
            pl.BlockSpec((1, D_MODEL), lambda i: (0, 0)),
            pl.BlockSpec((D_MODEL, 2 * CROSS_W), lambda i: (0, 0)),
        ],
        out_specs=[
            pl.BlockSpec((tm, CROSS_W), lambda i: (i, 0)),
            pl.BlockSpec((tm, CROSS_W), lambda i: (i, 0)),
        ],
        out_shape=[jax.ShapeDtypeStruct((n, CROSS_W), BF16)] * 2,
        compiler_params=pltpu.CompilerParams(
            dimension_semantics=("arbitrary",), vmem_limit_bytes=VMEM_LIMIT),
        name="mem_kv",
    )(mem2d, g, w_bf16)


def _rope_chunk(xc, cos, sin, first_half):
    rot = jnp.where(first_half, -pltpu.roll(xc, LANES - HEAD_DIM // 2, axis=1),
                    pltpu.roll(xc, HEAD_DIM // 2, axis=1))
    return xc * cos + rot * sin


def _proj_swa_kernel(h_ref, g_ref, w_ref, pos_ref, invf_ref, q_ref, k_ref, v_ref, qc_ref):
    xn = _rms(h_ref[...], g_ref[...]).astype(BF16)
    proj = _dot(xn, w_ref[...])
    ang = pos_ref[...].astype(F32) * invf_ref[...]
    cos = jnp.cos(ang)
    sin = jnp.sin(ang)
    first_half = (_lane_iota(cos.shape) % HEAD_DIM) < (HEAD_DIM // 2)
    for c in range(SELF_W // LANES):
        xc = proj[:, c * LANES:(c + 1) * LANES]
        q_ref[:, c * LANES:(c + 1) * LANES] = (
            _rope_chunk(xc, cos, sin, first_half) * Q_SCALE).astype(BF16)
    for c in range(KV_DUP_W // LANES):
        xc = proj[:, SELF_W + c * LANES:SELF_W + (c + 1) * LANES]
        k_ref[:, c * LANES:(c + 1) * LANES] = _rope_chunk(xc, cos, sin, first_half).astype(BF16)
    v0 = SELF_W + KV_DUP_W
    v_ref[...] = proj[:, v0:v0 + KV_DUP_W].astype(BF16)
    qc_ref[...] = (proj[:, v0 + KV_DUP_W:] * Q_SCALE).astype(BF16)


def _proj_swa(h, g, w_bf16, pos, invf, tm=512):
    n = h.shape[0]
    w_cols = w_bf16.shape[1]
    row = lambda i: (i, 0)
    fixed = lambda i: (0, 0)
    return pl.pallas_call(
        _proj_swa_kernel,
        grid=(n // tm,),
        in_specs=[
            pl.BlockSpec((tm, D_MODEL), row),
            pl.BlockSpec((1, D_MODEL), fixed),
            pl.BlockSpec((D_MODEL, w_cols), fixed),
            pl.BlockSpec((tm, 1), row),
            pl.BlockSpec((1, LANES), fixed),
        ],
        out_specs=[
            pl.BlockSpec((tm, SELF_W), row),
            pl.BlockSpec((tm, KV_DUP_W), row),
            pl.BlockSpec((tm, KV_DUP_W), row),
            pl.BlockSpec((tm, CROSS_W), row),
        ],
        out_shape=[
            jax.ShapeDtypeStruct((n, SELF_W), BF16),
            jax.ShapeDtypeStruct((n, KV_DUP_W), BF16),
            jax.ShapeDtypeStruct((n, KV_DUP_W), BF16),
            jax.ShapeDtypeStruct((n, CROSS_W), BF16),
        ],
        compiler_params=pltpu.CompilerParams(
            dimension_semantics=("arbitrary",), vmem_limit_bytes=VMEM_LIMIT),
        name="proj_swa",
    )(h, g, w_bf16, pos, invf)


def _proj_sb_kernel(h_ref, g_ref, w_ref, q_ref, k_ref, v_ref, qc_ref):
    xn = _rms(h_ref[...], g_ref[...]).astype(BF16)
    proj = _dot(xn, w_ref[...])
    q_ref[...] = (proj[:, :SELF_W] * (Q_SCALE * LOG2E)).astype(BF16)
    k_ref[...] = proj[:, SELF_W:2 * SELF_W].astype(BF16)
    v_ref[...] = proj[:, 2 * SELF_W:3 * SELF_W].astype(BF16)
    qc_ref[...] = (proj[:, 3 * SELF_W:] * Q_SCALE).astype(BF16)


def _proj_sb(h, g, w_bf16, tm=512):
    n = h.shape[0]
    row = lambda i: (i, 0)
    fixed = lambda i: (0, 0)
    return pl.pallas_call(
        _proj_sb_kernel,
        grid=(n // tm,),
        in_specs=[
            pl.BlockSpec((tm, D_MODEL), row),
            pl.BlockSpec((1, D_MODEL), fixed),
            pl.BlockSpec((D_MODEL, w_bf16.shape[1]), fixed),
        ],
        out_specs=[
            pl.BlockSpec((tm, SELF_W), row),
            pl.BlockSpec((tm, SELF_W), row),
            pl.BlockSpec((tm, SELF_W), row),
            pl.BlockSpec((tm, CROSS_W), row),
        ],
        out_shape=[
            jax.ShapeDtypeStruct((n, SELF_W), BF16),
            jax.ShapeDtypeStruct((n, SELF_W), BF16),
            jax.ShapeDtypeStruct((n, SELF_W), BF16),
            jax.ShapeDtypeStruct((n, CROSS_W), BF16),
        ],
        compiler_params=pltpu.CompilerParams(
            dimension_semantics=("arbitrary",), vmem_limit_bytes=VMEM_LIMIT),
        name="proj_sb",
    )(h, g, w_bf16)


def _cross_attention(qc_ref, km_ref, vm_ref, mixed_ref, tq):
    for p in range(CROSS_W // LANES):
        sl = slice(p * LANES, (p + 1) * LANES)
        qs = _half_masked_pair(qc_ref[:, sl])
        s = _dot_nt(qs, km_ref[:, sl])
        m = jnp.max(s, axis=-1, keepdims=True)
        e = jnp.exp(s - m)
        probs = (e / jnp.sum(e, axis=-1, keepdims=True)).astype(BF16)
        o = _dot(probs, vm_ref[:, sl])
        mixed_ref[:, SELF_W + p * LANES:SELF_W + (p + 1) * LANES] = _merge_halves(o, tq).astype(BF16)


def _out_proj_residual(mixed_ref, wo_ref, g_ref, h_ref):
    mixed = _dot(mixed_ref[...], wo_ref[...])
    return h_ref[...] + _rms(mixed, g_ref[...])


def _swa_kernel(sinks_ref, q_ref, k_ref, v_ref, qc_ref, km_ref, vm_ref, wo_ref, g_ref, h_ref,
                o_ref, mixed_ref, *, tq):
    i = pl.program_id(1)
    blocks = tq // BLOCK
    group_rows = 4 * BLOCK

    qi = lax.broadcasted_iota(jnp.int32, (group_rows, 2 * BLOCK), 0) % BLOCK
    kj = lax.broadcasted_iota(jnp.int32, (group_rows, 2 * BLOCK), 1)
    diff = qi + BLOCK - kj
    in_win = (diff >= 0) & (diff < SWA_WINDOW)
    row_head = lax.broadcasted_iota(jnp.int32, (group_rows, 1), 0) // BLOCK

    for blk in range(blocks):
        n = i * blocks + blk
        r0 = blk * BLOCK
        cur0 = pl.multiple_of(n * BLOCK, BLOCK)
        prev0 = pl.multiple_of(jnp.maximum(n - 1, 0) * BLOCK, BLOCK)
        mask = in_win & ((kj >= BLOCK) | (n > 0))
        for g in range(N_KV_HEADS_SWA):
            gs = slice(g * LANES, (g + 1) * LANES)
            kcat = jnp.concatenate([k_ref[pl.ds(prev0, BLOCK), gs], k_ref[pl.ds(cur0, BLOCK), gs]], axis=0)
            vcat = jnp.concatenate([v_ref[pl.ds(prev0, BLOCK), gs], v_ref[pl.ds(cur0, BLOCK), gs]], axis=0)
            c0 = g * 2 * LANES
            qs = jnp.concatenate(
                [_half_masked_pair(q_ref[r0:r0 + BLOCK, c0:c0 + LANES]),
                 _half_masked_pair(q_ref[r0:r0 + BLOCK, c0 + LANES:c0 + 2 * LANES])], axis=0)
            s = jnp.where(mask, _dot_nt(qs, kcat), NEG_BIG)
            sink = jnp.zeros((group_rows, 1), F32)
            for hh in range(4):
                sink = jnp.where(row_head == hh, sinks_ref[4 * g + hh], sink)
            m = jnp.maximum(jnp.max(s, axis=-1, keepdims=True), sink)
            e = jnp.exp(s - m)
            denom = jnp.sum(e, axis=-1, keepdims=True) + jnp.exp(sink - m)
            probs = (e / denom).astype(BF16)
            o = _dot(probs, vcat)
            mixed_ref[r0:r0 + BLOCK, c0:c0 + LANES] = _merge_halves(o[:2 * BLOCK], BLOCK).astype(BF16)
            mixed_ref[r0:r0 + BLOCK, c0 + LANES:c0 + 2 * LANES] = _merge_halves(o[2 * BLOCK:], BLOCK).astype(BF16)

    _cross_attention(qc_ref, km_ref, vm_ref, mixed_ref, tq)
    o_ref[...] = _out_proj_residual(mixed_ref, wo_ref, g_ref, h_ref)


def _swa_layer(sinks, q, k, v, qc, km, vm, wo_bf16, g, h, batch, seq, tq=512):
    n = h.shape[0]
    mem_len = km.shape[0] // batch
    tiles = seq // tq
    row = lambda b, i, s: (b * tiles + i, 0)
    per_b = lambda b, i, s: (b, 0)
    fixed = lambda b, i, s: (0, 0)
    grid_spec = pltpu.PrefetchScalarGridSpec(
        num_scalar_prefetch=1,
        grid=(batch, tiles),
        in_specs=[
            pl.BlockSpec((tq, SELF_W), row),
            pl.BlockSpec((seq, KV_DUP_W), per_b),
            pl.BlockSpec((seq, KV_DUP_W), per_b),
            pl.BlockSpec((tq, CROSS_W), row),
            pl.BlockSpec((mem_len, CROSS_W), per_b),
            pl.BlockSpec((mem_len, CROSS_W), per_b),
            pl.BlockSpec((D_MODEL, D_MODEL), fixed),
            pl.BlockSpec((1, D_MODEL), fixed),
            pl.BlockSpec((tq, D_MODEL), row),
        ],
        out_specs=pl.BlockSpec((tq, D_MODEL), row),
        scratch_shapes=[pltpu.VMEM((tq, D_MODEL), BF16)],
    )
    return pl.pallas_call(
        functools.partial(_swa_kernel, tq=tq),
        grid_spec=grid_spec,
        out_shape=jax.ShapeDtypeStruct((n, D_MODEL), F32),
        compiler_params=pltpu.CompilerParams(
            dimension_semantics=("arbitrary", "arbitrary"), vmem_limit_bytes=VMEM_LIMIT),
        name="swa_layer",
    )(sinks, q, k, v, qc, km, vm, wo_bf16, g, h)


def _sb_kernel(q_ref, k_ref, v_ref, qc_ref, km_ref, vm_ref, wo_ref, g_ref, h_ref, g2_ref, wr_ref,
               o_ref, xn_ref, route_ref, mixed_ref, qs_ref, acc_ref, carry_ref, *, tq):
    i = pl.program_id(1)
    tk = tq
    rows = 2 * tq
    pairs = SELF_W // LANES

    r_idx = lax.broadcasted_iota(jnp.int32, (rows, tk), 0) % tq
    c_idx = lax.broadcasted_iota(jnp.int32, (rows, tk), 1)
    causal = c_idx < r_idx
    kr = lax.broadcasted_iota(jnp.int32, (tk, tk), 0)
    kc = lax.broadcasted_iota(jnp.int32, (tk, tk), 1)
    suffix = jnp.where(kr >= kc, 1.0, 0.0).astype(BF16)

    for p in range(pairs):
        qs_ref[p] = _half_masked_pair(q_ref[:, p * LANES:(p + 1) * LANES])

    def block(j, diagonal):
        k0 = pl.multiple_of(j * tk, tk)
        for p in range(pairs):
            sl = slice(p * LANES, (p + 1) * LANES)
            z = _dot_nt(qs_ref[p], k_ref[pl.ds(k0, tk), sl])
            sp = jnp.maximum(z, 0.0) + jnp.log(1.0 + jnp.exp2(-jnp.abs(z))) * LOG2E
            if diagonal:
                sp = jnp.where(causal, sp, 0.0)
            s_incl = _dot(sp.astype(BF16), suffix)
            total = jnp.broadcast_to(s_incl[:, 0:1], (rows, LANES))
            if diagonal:
                w = jnp.where(causal, jnp.exp2(z - s_incl), 0.0)
                acc_ref[p] = _dot(w.astype(BF16), v_ref[pl.ds(k0, tk), sl])
                carry_ref[p] = total
            else:
                c = carry_ref[p]
                w = jnp.exp2(z - s_incl - jnp.concatenate([c] * (tk // LANES), axis=1))
                acc_ref[p] += _dot(w.astype(BF16), v_ref[pl.ds(k0, tk), sl])
                carry_ref[p] = c + total

    block(i, True)
    _cross_attention(qc_ref, km_ref, vm_ref, mixed_ref, tq)

    def unfinished(state):
        t, cmin = state
        return (t < i) & (cmin < SB_DONE_LOG2)

    def earlier_block(state):
        t, _ = state
        block(i - 1 - t, False)
        return t + 1, jnp.min(carry_ref[...])

    lax.while_loop(unfinished, earlier_block, (jnp.int32(0), jnp.min(carry_ref[...])))

    for p in range(pairs):
        mixed_ref[:, p * LANES:(p + 1) * LANES] = _merge_halves(acc_ref[p], tq).astype(BF16)

    h1 = _out_proj_residual(mixed_ref, wo_ref, g_ref, h_ref)
    o_ref[...] = h1

    xn = _rms(h1, g2_ref[...])
    xh = xn.astype(BF16)
    xn_ref[...] = xn
    xl = (xn - xh.astype(F32)).astype(BF16)
    wr = wr_ref[...]
    hi = _dot(xh, wr)
    logits = hi[:, :LANES] + hi[:, LANES:] + _dot(xl, wr[:, :LANES])
    lane = _lane_iota(logits.shape)
    l1 = jnp.where(lane < N_EXPERTS, logits, NEG_BIG)
    m1 = jnp.max(l1, axis=-1, keepdims=True)
    i1 = jnp.min(jnp.where(l1 == m1, lane, LANES), axis=-1, keepdims=True)
    l2 = jnp.where(lane == i1, NEG_BIG, l1)
    m2 = jnp.max(l2, axis=-1, keepdims=True)
    i2 = jnp.min(jnp.where(l2 == m2, lane, LANES), axis=-1, keepdims=True)
    e2 = jnp.exp(m2 - m1)
    w1 = 1.0 / (1.0 + e2)
    w2 = e2 / (1.0 + e2)
    lane8 = _lane_iota((tq, 8))
    route_ref[...] = jnp.where(lane8 == 0, i1.astype(F32),
                     jnp.where(lane8 == 1, i2.astype(F32),
                     jnp.where(lane8 == 2, w1, jnp.where(lane8 == 3, w2, 0.0))))


def _sb_layer(q, k, v, qc, km, vm, wo_bf16, g, h, g2, wr_split, batch, seq, tq=256):
    n = h.shape[0]
    mem_len = km.shape[0] // batch
    tiles = seq // tq
    row = lambda b, i: (b * tiles + i, 0)
    per_b = lambda b, i: (b, 0)
    fixed = lambda b, i: (0, 0)
    return pl.pallas_call(
        functools.partial(_sb_kernel, tq=tq),
        grid=(batch, tiles),
        in_specs=[
            pl.BlockSpec((tq, SELF_W), row),
            pl.BlockSpec((seq, SELF_W), per_b),
            pl.BlockSpec((seq, SELF_W), per_b),
            pl.BlockSpec((tq, CROSS_W), row),
            pl.BlockSpec((mem_len, CROSS_W), per_b),
            pl.BlockSpec((mem_len, CROSS_W), per_b),
            pl.BlockSpec((D_MODEL, D_MODEL), fixed),
            pl.BlockSpec((1, D_MODEL), fixed),
            pl.BlockSpec((tq, D_MODEL), row),
            pl.BlockSpec((1, D_MODEL), fixed),
            pl.BlockSpec((D_MODEL, 2 * LANES), fixed),
        ],
        out_specs=[
            pl.BlockSpec((tq, D_MODEL), row),
            pl.BlockSpec((tq, D_MODEL), row),
            pl.BlockSpec((tq, 8), row),
        ],
        out_shape=[
            jax.ShapeDtypeStruct((n, D_MODEL), F32),
            jax.ShapeDtypeStruct((n, D_MODEL), F32),
            jax.ShapeDtypeStruct((n, 8), F32),
        ],
        scratch_shapes=[
            pltpu.VMEM((tq, D_MODEL), BF16),
            pltpu.VMEM((SELF_W // LANES, 2 * tq, LANES), BF16),
            pltpu.VMEM((SELF_W // LANES, 2 * tq, LANES), F32),
            pltpu.VMEM((SELF_W // LANES, 2 * tq, LANES), F32),
        ],
        compiler_params=pltpu.CompilerParams(
            dimension_semantics=("arbitrary", "arbitrary"), vmem_limit_bytes=VMEM_LIMIT),
        name="sb_layer",
    )(q, k, v, qc, km, vm, wo_bf16, g, h, g2, wr_split)


def _swiglu_part(x_bf16, wg_ref, wu_ref, wd_ref):
    gate = _dot(x_bf16, wg_ref[...].astype(BF16))
    up = _dot(x_bf16, wu_ref[...].astype(BF16))
    act = (gate * jax.nn.sigmoid(gate) * up).astype(BF16)
    return _dot(act, wd_ref[...].astype(BF16))


def _dense_ffn_kernel(h_ref, g1_ref, wg_ref, wu_ref, wd_ref, g2_ref, o_ref, xn_ref):
    f = pl.program_id(1)

    @pl.when(f == 0)
    def _():
        xn_ref[...] = _rms(h_ref[...], g1_ref[...]).astype(BF16)
        o_ref[...] = jnp.zeros_like(o_ref)

    o_ref[...] += _swiglu_part(xn_ref[...], wg_ref, wu_ref, wd_ref)

    @pl.when(f == pl.num_programs(1) - 1)
    def _():
        o_ref[...] = h_ref[...] + _rms(o_ref[...], g2_ref[...])


def _dense_ffn(h, g1, wg, wu, wd, g2, tm=1024, tf=512):
    n = h.shape[0]
    row = lambda i, f: (i, 0)
    fixed = lambda i, f: (0, 0)
    return pl.pallas_call(
        _dense_ffn_kernel,
        grid=(n // tm, D_FF // tf),
        in_specs=[
            pl.BlockSpec((tm, D_MODEL), row),
            pl.BlockSpec((1, D_MODEL), fixed),
            pl.BlockSpec((D_MODEL, tf), lambda i, f: (0, f)),
            pl.BlockSpec((D_MODEL, tf), lambda i, f: (0, f)),
            pl.BlockSpec((tf, D_MODEL), lambda i, f: (f, 0)),
            pl.BlockSpec((1, D_MODEL), fixed),
        ],
        out_specs=pl.BlockSpec((tm, D_MODEL), row),
        out_shape=jax.ShapeDtypeStruct((n, D_MODEL), F32),
        scratch_shapes=[pltpu.VMEM((tm, D_MODEL), BF16)],
        compiler_params=pltpu.CompilerParams(
            dimension_semantics=("arbitrary", "arbitrary"), vmem_limit_bytes=VMEM_LIMIT),
        name="dense_ffn",
    )(h, g1, wg, wu, wd, g2)


def _moe_ffn_kernel(te_ref, used_ref, xs_ref, wg_ref, wu_ref, wd_ref, o_ref, xb_ref):
    i = pl.program_id(0)
    f = pl.program_id(1)

    @pl.when(i < used_ref[0])
    def _():
        @pl.when(f == 0)
        def _():
            xb_ref[...] = xs_ref[...].astype(BF16)
            o_ref[...] = jnp.zeros_like(o_ref)

        o_ref[...] += _swiglu_part(xb_ref[...], wg_ref, wu_ref, wd_ref)

    @pl.when((i >= used_ref[0]) & (f == pl.num_programs(1) - 1))
    def _():
        o_ref[...] = jnp.zeros_like(o_ref)


def _moe_ffn(tile_expert, used, xs, wg, wu, wd, tm, tf=512):
    p_rows = xs.shape[0]
    nf = D_FF // tf

    def f_eff(i, f, used_ref):
        return jnp.where(i < used_ref[0], f, nf - 1)

    grid_spec = pltpu.PrefetchScalarGridSpec(
        num_scalar_prefetch=2,
        grid=(p_rows // tm, nf),
        in_specs=[
            pl.BlockSpec((tm, D_MODEL), lambda i, f, te, u: (jnp.minimum(i, u[0] - 1), 0)),
            pl.BlockSpec((None, D_MODEL, tf), lambda i, f, te, u: (te[i], 0, f_eff(i, f, u))),
            pl.BlockSpec((None, D_MODEL, tf), lambda i, f, te, u: (te[i], 0, f_eff(i, f, u))),
            pl.BlockSpec((None, tf, D_MODEL), lambda i, f, te, u: (te[i], f_eff(i, f, u), 0)),
        ],
        out_specs=pl.BlockSpec((tm, D_MODEL), lambda i, f, te, u: (i, 0)),
        scratch_shapes=[pltpu.VMEM((tm, D_MODEL), BF16)],
    )
    return pl.pallas_call(
        _moe_ffn_kernel,
        grid_spec=grid_spec,
        out_shape=jax.ShapeDtypeStruct((p_rows, D_MODEL), F32),
        compiler_params=pltpu.CompilerParams(
            dimension_semantics=("arbitrary", "arbitrary"), vmem_limit_bytes=VMEM_LIMIT),
        name="moe_ffn",
    )(tile_expert, used, xs, wg, wu, wd)


def _dispatch_kernel(slot_ref, last_ref, xn_ref, xs_hbm, zero_ref, sems, *, tt, tm):
    i = pl.program_id(0)
    p_rows = xs_hbm.shape[0]

    def zero_tiles():
        for e in range(N_EXPERTS):
            yield last_ref[e], last_ref[e] >= 0
        for u in range(N_EXPERTS):
            start = last_ref[N_EXPERTS] + u * tm
            yield start, start < p_rows

    def zero_copy(start):
        return pltpu.make_async_copy(zero_ref, xs_hbm.at[pl.ds(pl.multiple_of(start, tm), tm)], sems.at[1])

    @pl.when(i == 0)
    def _():
        zero_ref[...] = jnp.zeros_like(zero_ref)
        for start, exists in zero_tiles():
            @pl.when(exists)
            def _():
                zero_copy(start).start()
        for start, exists in zero_tiles():
            @pl.when(exists)
            def _():
                zero_copy(start).wait()

    sem = sems.at[0]

    def issue(r, carry):
        t = i * tt + r
        src = xn_ref.at[pl.ds(r, 1)]
        pltpu.make_async_copy(src, xs_hbm.at[pl.ds(slot_ref[2 * t], 1)], sem).start()
        pltpu.make_async_copy(src, xs_hbm.at[pl.ds(slot_ref[2 * t + 1], 1)], sem).start()
        return carry

    lax.fori_loop(0, tt, issue, 0, unroll=True)

    for _ in range(2):
        pltpu.make_async_copy(xn_ref, xs_hbm.at[pl.ds(0, tt)], sem).wait()


def _dispatch(slot_flat, last_tile_start, xn, p_rows, tm, tt=1024):
    n = xn.shape[0]
    grid_spec = pltpu.PrefetchScalarGridSpec(
        num_scalar_prefetch=2,
        grid=(n // tt,),
        in_specs=[pl.BlockSpec((tt, D_MODEL), lambda i, s, l: (i, 0))],
        out_specs=pl.BlockSpec(memory_space=pl.ANY),
        scratch_shapes=[pltpu.VMEM((tm, D_MODEL), F32), pltpu.SemaphoreType.DMA((2,))],
    )
    return pl.pallas_call(
        functools.partial(_dispatch_kernel, tt=tt, tm=tm),
        grid_spec=grid_spec,
        out_shape=jax.ShapeDtypeStruct((p_rows, D_MODEL), F32),
        compiler_params=pltpu.CompilerParams(
            dimension_semantics=("arbitrary",), vmem_limit_bytes=VMEM_LIMIT),
        name="moe_dispatch",
    )(slot_flat, last_tile_start, xn)


def _combine_kernel(slot_ref, h_ref, route_ref, g_ref, ys_hbm, o_ref, ybuf, sems, *, tt):
    i = pl.program_id(0)
    steps = pl.num_programs(0)

    def issue_step(step, buf):
        def issue(r, carry):
            t = step * tt + r
            for k in range(2):
                pltpu.make_async_copy(ys_hbm.at[pl.ds(slot_ref[2 * t + k], 1)],
                                      ybuf.at[buf, k, pl.ds(r, 1)], sems.at[buf]).start()
            return carry
        lax.fori_loop(0, tt, issue, 0, unroll=True)

    @pl.when(i == 0)
    def _():
        issue_step(0, 0)

    @pl.when(i + 1 < steps)
    def _():
        issue_step(i + 1, (i + 1) % 2)

    buf = i % 2
    for k in range(2):
        pltpu.make_async_copy(ys_hbm.at[pl.ds(0, tt)], ybuf.at[buf, k], sems.at[buf]).wait()
    r = route_ref[...]
    y = r[:, 2:3] * ybuf[buf, 0] + r[:, 3:4] * ybuf[buf, 1]
    o_ref[...] = h_ref[...] + _rms(y, g_ref[...])


def _combine(slot_flat, h, route, g, ys, tt=256):
    n = h.shape[0]
    row = lambda i, s: (i, 0)
    grid_spec = pltpu.PrefetchScalarGridSpec(
        num_scalar_prefetch=1,
        grid=(n // tt,),
        in_specs=[
            pl.BlockSpec((tt, D_MODEL), row),
            pl.BlockSpec((tt, 8), row),
            pl.BlockSpec((1, D_MODEL), lambda i, s: (0, 0)),
            pl.BlockSpec(memory_space=pl.ANY),
        ],
        out_specs=pl.BlockSpec((tt, D_MODEL), row),
        scratch_shapes=[pltpu.VMEM((2, 2, tt, D_MODEL), F32), pltpu.SemaphoreType.DMA((2,))],
    )
    return pl.pallas_call(
        functools.partial(_combine_kernel, tt=tt),
        grid_spec=grid_spec,
        out_shape=jax.ShapeDtypeStruct((n, D_MODEL), F32),
        compiler_params=pltpu.CompilerParams(
            dimension_semantics=("arbitrary",), vmem_limit_bytes=VMEM_LIMIT),
        name="moe_combine",
    )(slot_flat, h, route, g, ys)


def _dispatch_plan(route, tm):
    n = route.shape[0]
    blk = 256
    ex = route[:, :2].astype(jnp.int32)
    eids = jnp.arange(N_EXPERTS, dtype=jnp.int32)[None, :]
    member = (ex[:, 0:1] == eids) | (ex[:, 1:2] == eids)
    mb = member.reshape(n // blk, blk, N_EXPERTS).astype(BF16)
    before = (jnp.arange(blk)[:, None] > jnp.arange(blk)[None, :]).astype(BF16)
    local = jnp.einsum('ij,bjk->bik', before, mb, preferred_element_type=F32)
    tot = jnp.sum(mb.astype(F32), axis=1)
    base = jnp.cumsum(tot, axis=0) - tot
    rank = (local + base[:, None, :]).reshape(n, N_EXPERTS).astype(jnp.int32)
    counts = jnp.sum(tot, axis=0).astype(jnp.int32)
    padded = ((counts + tm - 1) // tm) * tm
    ends = jnp.cumsum(padded)
    slot_all = (ends - padded)[None, :] + rank
    slot = jnp.stack([jnp.sum(jnp.where(ex[:, k:k + 1] == eids, slot_all, 0), axis=1) for k in range(2)], axis=1)
    p_rows = 2 * n + N_EXPERTS * tm
    last_tile_start = jnp.concatenate([jnp.where(padded > 0, ends - tm, -1), ends[-1:]]).astype(jnp.int32)
    tile_start = jnp.arange(p_rows // tm, dtype=jnp.int32) * tm
    tile_expert = jnp.sum((tile_start[:, None] >= ends[None, :]).astype(jnp.int32), axis=1)
    used = (ends[-1] // tm).astype(jnp.int32)
    last_expert = jnp.take(tile_expert, jnp.maximum(used - 1, 0))
    tile_expert = jnp.where(tile_start // tm < used, tile_expert, last_expert).astype(jnp.int32)
    return slot.reshape(-1), last_tile_start, tile_expert, used.reshape(1), p_rows


def kernel(x, mem, positions, l0_attn_pre_g, l0_mem_g, l0_w_in, l0_w_mem_kv, l0_sinks, l0_w_out, l0_attn_post_g, l0_ffn_pre_g, l0_w_gate, l0_w_up, l0_w_down, l0_ffn_post_g, l1_attn_pre_g, l1_mem_g, l1_w_in, l1_w_mem_kv, l1_w_out, l1_attn_post_g, l1_ffn_pre_g, l1_w_router, l1_w_gate, l1_w_up, l1_w_down, l1_ffn_post_g):
    batch, seq, _ = x.shape
    n = batch * seq
    h = x.reshape(n, D_MODEL)
    mem2d = mem.reshape(-1, D_MODEL)
    row = lambda g: g.reshape(1, D_MODEL)

    kv0 = SELF_W
    kw = N_KV_HEADS_SWA * HEAD_DIM
    dup = lambda w: jnp.tile(w.reshape(D_MODEL, N_KV_HEADS_SWA, 1, HEAD_DIM), (1, 1, 2, 1)).reshape(D_MODEL, KV_DUP_W)
    w_in0 = jnp.concatenate(
        [l0_w_in[:, :kv0], dup(l0_w_in[:, kv0:kv0 + kw]), dup(l0_w_in[:, kv0 + kw:kv0 + 2 * kw]),
         l0_w_in[:, kv0 + 2 * kw:]], axis=1).astype(BF16)
    inv_freq = ROPE_THETA ** (-jnp.arange(0, HEAD_DIM, 2, dtype=F32) / HEAD_DIM)
    invf = jnp.tile(inv_freq, LANES // (HEAD_DIM // 2)).reshape(1, LANES)

    km0, vm0 = _mem_kv(mem2d, row(l0_mem_g), l0_w_mem_kv.astype(BF16))
    q, k, v, qc = _proj_swa(h, row(l0_attn_pre_g), w_in0, positions.reshape(n, 1), invf)
    h = _swa_layer(l0_sinks, q, k, v, qc, km0, vm0, l0_w_out.astype(BF16), row(l0_attn_post_g), h, batch, seq)
    h = _dense_ffn(h, row(l0_ffn_pre_g), l0_w_gate, l0_w_up, l0_w_down, row(l0_ffn_post_g))

    km1, vm1 = _mem_kv(mem2d, row(l1_mem_g), l1_w_mem_kv.astype(BF16))
    q, k, v, qc = _proj_sb(h, row(l1_attn_pre_g), l1_w_in.astype(BF16))
    wr_pad = jnp.pad(l1_w_router, ((0, 0), (0, LANES - N_EXPERTS)))
    wr_hi = wr_pad.astype(BF16)
    wr_split = jnp.concatenate([wr_hi, (wr_pad - wr_hi.astype(F32)).astype(BF16)], axis=1)
    h, xn, route = _sb_layer(q, k, v, qc, km1, vm1, l1_w_out.astype(BF16), row(l1_attn_post_g), h,
                             row(l1_ffn_pre_g), wr_split, batch, seq)

    tm = 1024
    slot_flat, last_tile_start, tile_expert, used, p_rows = _dispatch_plan(route, tm)
    xs = _dispatch(slot_flat, last_tile_start, xn, p_rows, tm)
    ys = _moe_ffn(tile_expert, used, xs, l1_w_gate, l1_w_up, l1_w_down, tm)
    h = _combine(slot_flat, h, route, row(l1_ffn_post_g), ys)
    return h.reshape(batch, seq, D_MODEL)
```

```python
import functools

import jax
import jax.numpy as jnp
from jax import lax
from jax.experimental import pallas as pl
from jax.experimental.pallas import tpu as pltpu

D_MODEL = 1024
HEAD_DIM = 64
N_SELF_HEADS = 12
N_KV_HEADS_SWA = 3
N_CROSS_HEADS = 4
SWA_WINDOW = 128
BLOCK = 128
ROPE_THETA = 10000.0
D_FF = 3584
N_EXPERTS = 8
RMS_EPS = 1e-6

SELF_W = N_SELF_HEADS * HEAD_DIM
CROSS_W = N_CROSS_HEADS * HEAD_DIM
KV_DUP_W = N_KV_HEADS_SWA * 2 * HEAD_DIM
LANES = 128
Q_SCALE = HEAD_DIM ** -0.5
NEG_BIG = -1e30
LOG2E = 1.4426950408889634
SB_DONE_LOG2 = 160.0

VMEM_LIMIT = 56 * 1024 * 1024

F32 = jnp.float32
BF16 = jnp.bfloat16


def _rms(xf, g):
    var = jnp.mean(xf * xf, axis=-1, keepdims=True)
    return xf * lax.rsqrt(var + RMS_EPS) * g


def _lane_iota(shape):
    return lax.broadcasted_iota(jnp.int32, shape, len(shape) - 1)


def _dot(a, b):
    return jnp.dot(a, b, preferred_element_type=F32)


def _dot_nt(a, b):
    return lax.dot_general(a, b, (((1,), (1,)), ((), ())), preferred_element_type=F32)


def _half_masked_pair(x2):
    lo = _lane_iota(x2.shape) < HEAD_DIM
    zero = jnp.zeros_like(x2)
    return jnp.concatenate([jnp.where(lo, x2, zero), jnp.where(lo, zero, x2)], axis=0)


def _merge_halves(o, r):
    lo = _lane_iota((r, LANES)) < HEAD_DIM
    return jnp.where(lo, o[:r], o[r:])


def _mem_kv_kernel(mem_ref, g_ref, w_ref, k_ref, v_ref):
    xn = _rms(mem_ref[...], g_ref[...]).astype(BF16)
    kv = _dot(xn, w_ref[...])
    k_ref[...] = kv[:, :CROSS_W].astype(BF16)
    v_ref[...] = kv[:, CROSS_W:].astype(BF16)


def _mem_kv(mem2d, g, w_bf16, tm=512):
    n = mem2d.shape[0]
    return pl.pallas_call(
        _mem_kv_kernel,
        grid=(n // tm,),
        in_specs=[
            pl.BlockSpec((tm, D_MODEL), lambda i: (i, 0)),
            pl.BlockSpec((1, D_MODEL), lambda i: (0, 0)),
            pl.BlockSpec((D_MODEL, 2 * CROSS_W), lambda i: (0, 0)),
        ],
        out_specs=[
            pl.BlockSpec((tm, CROSS_W), lambda i: (i, 0)),
            pl.BlockSpec((tm, CROSS_W), lambda i: (i, 0)),
        ],
        out_shape=[jax.ShapeDtypeStruct((n, CROSS_W), BF16)] * 2,
        compiler_params=pltpu.CompilerParams(
            dimension_semantics=("arbitrary",), vmem_limit_bytes=VMEM_LIMIT),
        name="mem_kv",
    )(mem2d, g, w_bf16)


def _rope_chunk(xc, cos, sin, first_half):
    rot = jnp.where(first_half, -pltpu.roll(xc, LANES - HEAD_DIM // 2, axis=1),
                    pltpu.roll(xc, HEAD_DIM // 2, axis=1))
    return xc * cos + rot * sin


def _proj_swa_kernel(h_ref, g_ref, w_ref, pos_ref, invf_ref, q_ref, k_ref, v_ref, qc_ref):
    xn = _rms(h_ref[...], g_ref[...]).astype(BF16)
    proj = _dot(xn, w_ref[...])
    ang = pos_ref[...].astype(F32) * invf_ref[...]
    cos = jnp.cos(ang)
    sin = jnp.sin(ang)
    first_half = (_lane_iota(cos.shape) % HEAD_DIM) < (HEAD_DIM // 2)
    for c in range(SELF_W // LANES):
        xc = proj[:, c * LANES:(c + 1) * LANES]
        q_ref[:, c * LANES:(c + 1) * LANES] = (
            _rope_chunk(xc, cos, sin, first_half) * Q_SCALE).astype(BF16)
    for c in range(KV_DUP_W // LANES):
        xc = proj[:, SELF_W + c * LANES:SELF_W + (c + 1) * LANES]
        k_ref[:, c * LANES:(c + 1) * LANES] = _rope_chunk(xc, cos, sin, first_half).astype(BF16)
    v0 = SELF_W + KV_DUP_W
    v_ref[...] = proj[:, v0:v0 + KV_DUP_W].astype(BF16)
    qc_ref[...] = (proj[:, v0 + KV_DUP_W:] * Q_SCALE).astype(BF16)


def _proj_swa(h, g, w_bf16, pos, invf, tm=512):
    n = h.shape[0]
    w_cols = w_bf16.shape[1]
    row = lambda i: (i, 0)
    fixed = lambda i: (0, 0)
    return pl.pallas_call(
        _proj_swa_kernel,
        grid=(n // tm,),
        in_specs=[
            pl.BlockSpec((tm, D_MODEL), row),
            pl.BlockSpec((1, D_MODEL), fixed),
            pl.BlockSpec((D_MODEL, w_cols), fixed),
            pl.BlockSpec((tm, 1), row),
            pl.BlockSpec((1, LANES), fixed),
        ],
        out_specs=[
            pl.BlockSpec((tm, SELF_W), row),
            pl.BlockSpec((tm, KV_DUP_W), row),
            pl.BlockSpec((tm, KV_DUP_W), row),
            pl.BlockSpec((tm, CROSS_W), row),
        ],
        out_shape=[
            jax.ShapeDtypeStruct((n, SELF_W), BF16),
            jax.ShapeDtypeStruct((n, KV_DUP_W), BF16),
            jax.ShapeDtypeStruct((n, KV_DUP_W), BF16),
            jax.ShapeDtypeStruct((n, CROSS_W), BF16),
        ],
        compiler_params=pltpu.CompilerParams(
            dimension_semantics=("arbitrary",), vmem_limit_bytes=VMEM_LIMIT),
        name="proj_swa",
    )(h, g, w_bf16, pos, invf)


def _proj_sb_kernel(h_ref, g_ref, w_ref, q_ref, k_ref, v_ref, qc_ref):
    xn = _rms(h_ref[...], g_ref[...]).astype(BF16)
    proj = _dot(xn, w_ref[...])
    q_ref[...] = (proj[:, :SELF_W] * (Q_SCALE * LOG2E)).astype(BF16)
    k_ref[...] = proj[:, SELF_W:2 * SELF_W].astype(BF16)
    v_ref[...] = proj[:, 2 * SELF_W:3 * SELF_W].astype(BF16)
    qc_ref[...] = (proj[:, 3 * SELF_W:] * Q_SCALE).astype(BF16)


def _proj_sb(h, g, w_bf16, tm=512):
    n = h.shape[0]
    row = lambda i: (i, 0)
    fixed = lambda i: (0, 0)
    return pl.pallas_call(
        _proj_sb_kernel,
        grid=(n // tm,),
        in_specs=[
            pl.BlockSpec((tm, D_MODEL), row),
            pl.BlockSpec((1, D_MODEL), fixed),
            pl.BlockSpec((D_MODEL, w_bf16.shape[1]), fixed),
        ],
        out_specs=[
            pl.BlockSpec((tm, SELF_W), row),
            pl.BlockSpec((tm, SELF_W), row),
            pl.BlockSpec((tm, SELF_W), row),
            pl.BlockSpec((tm, CROSS_W), row),
        ],
        out_shape=[
            jax.ShapeDtypeStruct((n, SELF_W), BF16),
            jax.ShapeDtypeStruct((n, SELF_W), BF16),
            jax.ShapeDtypeStruct((n, SELF_W), BF16),
            jax.ShapeDtypeStruct((n, CROSS_W), BF16),
        ],
        compiler_params=pltpu.CompilerParams(
            dimension_semantics=("arbitrary",), vmem_limit_bytes=VMEM_LIMIT),
        name="proj_sb",
    )(h, g, w_bf16)


def _cross_attention(qc_ref, km_ref, vm_ref, mixed_ref, tq):
    for p in range(CROSS_W // LANES):
        sl = slice(p * LANES, (p + 1) * LANES)
        qs = _half_masked_pair(qc_ref[:, sl])
        s = _dot_nt(qs, km_ref[:, sl])
        m = jnp.max(s, axis=-1, keepdims=True)
        e = jnp.exp(s - m)
        probs = (e / jnp.sum(e, axis=-1, keepdims=True)).astype(BF16)
        o = _dot(probs, vm_ref[:, sl])
        mixed_ref[:, SELF_W + p * LANES:SELF_W + (p + 1) * LANES] = _merge_halves(o, tq).astype(BF16)


def _out_proj_residual(mixed_ref, wo_ref, g_ref, h_ref):
    mixed = _dot(mixed_ref[...], wo_ref[...])
    return h_ref[...] + _rms(mixed, g_ref[...])


def _swa_kernel(sinks_ref, q_ref, k_ref, v_ref, qc_ref, km_ref, vm_ref, wo_ref, g_ref, h_ref,
                o_ref, mixed_ref, *, tq):
    i = pl.program_id(1)
    blocks = tq // BLOCK
    group_rows = 4 * BLOCK

    qi = lax.broadcasted_iota(jnp.int32, (group_rows, 2 * BLOCK), 0) % BLOCK
    kj = lax.broadcasted_iota(jnp.int32, (group_rows, 2 * BLOCK), 1)
    diff = qi + BLOCK - kj
    in_win = (diff >= 0) & (diff < SWA_WINDOW)
    row_head = lax.broadcasted_iota(jnp.int32, (group_rows, 1), 0) // BLOCK

    for blk in range(blocks):
        n = i * blocks + blk
        r0 = blk * BLOCK
        cur0 = pl.multiple_of(n * BLOCK, BLOCK)
        prev0 = pl.multiple_of(jnp.maximum(n - 1, 0) * BLOCK, BLOCK)
        mask = in_win & ((kj >= BLOCK) | (n > 0))
        for g in range(N_KV_HEADS_SWA):
            gs = slice(g * LANES, (g + 1) * LANES)
            kcat = jnp.concatenate([k_ref[pl.ds(prev0, BLOCK), gs], k_ref[pl.ds(cur0, BLOCK), gs]], axis=0)
            vcat = jnp.concatenate([v_ref[pl.ds(prev0, BLOCK), gs], v_ref[pl.ds(cur0, BLOCK), gs]], axis=0)
            c0 = g * 2 * LANES
            qs = jnp.concatenate(
                [_half_masked_pair(q_ref[r0:r0 + BLOCK, c0:c0 + LANES]),
                 _half_masked_pair(q_ref[r0:r0 + BLOCK, c0 + LANES:c0 + 2 * LANES])], axis=0)
            s = jnp.where(mask, _dot_nt(qs, kcat), NEG_BIG)
            sink = jnp.zeros((group_rows, 1), F32)
            for hh in range(4):
                sink = jnp.where(row_head == hh, sinks_ref[4 * g + hh], sink)
            m = jnp.maximum(jnp.max(s, axis=-1, keepdims=True), sink)
            e = jnp.exp(s - m)
            denom = jnp.sum(e, axis=-1, keepdims=True) + jnp.exp(sink - m)
            probs = (e / denom).astype(BF16)
            o = _dot(probs, vcat)
            mixed_ref[r0:r0 + BLOCK, c0:c0 + LANES] = _merge_halves(o[:2 * BLOCK], BLOCK).astype(BF16)
            mixed_ref[r0:r0 + BLOCK, c0 + LANES:c0 + 2 * LANES] = _merge_halves(o[2 * BLOCK:], BLOCK).astype(BF16)

    _cross_attention(qc_ref, km_ref, vm_ref, mixed_ref, tq)
    o_ref[...] = _out_proj_residual(mixed_ref, wo_ref, g_ref, h_ref)


def _swa_layer(sinks, q, k, v, qc, km, vm, wo_bf16, g, h, batch, seq, tq=512):
    n = h.shape[0]
    mem_len = km.shape[0] // batch
    tiles = seq // tq
    row = lambda b, i, s: (b * tiles + i, 0)
    per_b = lambda b, i, s: (b, 0)
    fixed = lambda b, i, s: (0, 0)
    grid_spec = pltpu.PrefetchScalarGridSpec(
        num_scalar_prefetch=1,
        grid=(batch, tiles),
        in_specs=[
            pl.BlockSpec((tq, SELF_W), row),
            pl.BlockSpec((seq, KV_DUP_W), per_b),
            pl.BlockSpec((seq, KV_DUP_W), per_b),
            pl.BlockSpec((tq, CROSS_W), row),
            pl.BlockSpec((mem_len, CROSS_W), per_b),
            pl.BlockSpec((mem_len, CROSS_W), per_b),
            pl.BlockSpec((D_MODEL, D_MODEL), fixed),
            pl.BlockSpec((1, D_MODEL), fixed),
            pl.BlockSpec((tq, D_MODEL), row),
        ],
        out_specs=pl.BlockSpec((tq, D_MODEL), row),
        scratch_shapes=[pltpu.VMEM((tq, D_MODEL), BF16)],
    )
    return pl.pallas_call(
        functools.partial(_swa_kernel, tq=tq),
        grid_spec=grid_spec,
        out_shape=jax.ShapeDtypeStruct((n, D_MODEL), F32),
        compiler_params=pltpu.CompilerParams(
            dimension_semantics=("arbitrary", "arbitrary"), vmem_limit_bytes=VMEM_LIMIT),
        name="swa_layer",
    )(sinks, q, k, v, qc, km, vm, wo_bf16, g, h)


def _sb_kernel(q_ref, k_ref, v_ref, qc_ref, km_ref, vm_ref, wo_ref, g_ref, h_ref, g2_ref, wr_ref,
               o_ref, xn_ref, route_ref, mixed_ref, qs_ref, acc_ref, carry_ref, *, tq):
    i = pl.program_id(1)
    ts = tq // 2
    tk = ts
    rows = 2 * ts
    pairs = SELF_W // LANES
    first = 2 * i

    r_idx = lax.broadcasted_iota(jnp.int32, (rows, tk), 0) % ts
    c_idx = lax.broadcasted_iota(jnp.int32, (rows, tk), 1)
    causal = c_idx < r_idx
    kr = lax.broadcasted_iota(jnp.int32, (tk, tk), 0)
    kc = lax.broadcasted_iota(jnp.int32, (tk, tk), 1)
    suffix = jnp.where(kr >= kc, 1.0, 0.0).astype(BF16)

    for u in range(2):
        for p in range(pairs):
            qs_ref[u, p] = _half_masked_pair(q_ref[u * ts:(u + 1) * ts, p * LANES:(p + 1) * LANES])

    def block(u, j, diagonal):
        k0 = pl.multiple_of(j * tk, tk)
        for p in range(pairs):
            sl = slice(p * LANES, (p + 1) * LANES)
            z = _dot_nt(qs_ref[u, p], k_ref[pl.ds(k0, tk), sl])
            sp = jnp.maximum(z, 0.0) + jnp.log(1.0 + jnp.exp2(-jnp.abs(z))) * LOG2E
            if diagonal:
                sp = jnp.where(causal, sp, 0.0)
            s_incl = _dot(sp.astype(BF16), suffix)
            total = jnp.broadcast_to(s_incl[:, 0:1], (rows, LANES))
            if diagonal:
                w = jnp.where(causal, jnp.exp2(z - s_incl), 0.0)
                acc_ref[u, p] = _dot(w.astype(BF16), v_ref[pl.ds(k0, tk), sl])
                carry_ref[u, p] = total
            else:
                c = carry_ref[u, p]
                w = jnp.exp2(z - s_incl - jnp.concatenate([c] * (tk // LANES), axis=1))
                acc_ref[u, p] += _dot(w.astype(BF16), v_ref[pl.ds(k0, tk), sl])
                carry_ref[u, p] = c + total

    for u in range(2):
        block(u, first + u, True)
    _cross_attention(qc_ref, km_ref, vm_ref, mixed_ref, tq)

    def unfinished(state):
        t, cmin = state
        return (t < first) & (cmin < SB_DONE_LOG2)

    def earlier_blocks(state):
        t, _ = state
        for u in range(2):
            block(u, first + u - 1 - t, False)
        return t + 1, jnp.min(carry_ref[...])

    t_done, _ = lax.while_loop(unfinished, earlier_blocks, (jnp.int32(0), jnp.min(carry_ref[...])))

    @pl.when((t_done == first) & (jnp.min(carry_ref[1]) < SB_DONE_LOG2))
    def _():
        block(1, 0, False)

    for u in range(2):
        for p in range(pairs):
            mixed_ref[u * ts:(u + 1) * ts, p * LANES:(p + 1) * LANES] = _merge_halves(acc_ref[u, p], ts).astype(BF16)

    h1 = _out_proj_residual(mixed_ref, wo_ref, g_ref, h_ref)
    o_ref[...] = h1

    xn = _rms(h1, g2_ref[...])
    xh = xn.astype(BF16)
    xn_ref[...] = xn
    xl = (xn - xh.astype(F32)).astype(BF16)
    wr = wr_ref[...]
    hi = _dot(xh, wr)
    logits = hi[:, :LANES] + hi[:, LANES:] + _dot(xl, wr[:, :LANES])
    lane = _lane_iota(logits.shape)
    l1 = jnp.where(lane < N_EXPERTS, logits, NEG_BIG)
    m1 = jnp.max(l1, axis=-1, keepdims=True)
    i1 = jnp.min(jnp.where(l1 == m1, lane, LANES), axis=-1, keepdims=True)
    l2 = jnp.where(lane == i1, NEG_BIG, l1)
    m2 = jnp.max(l2, axis=-1, keepdims=True)
    i2 = jnp.min(jnp.where(l2 == m2, lane, LANES), axis=-1, keepdims=True)
    e2 = jnp.exp(m2 - m1)
    w1 = 1.0 / (1.0 + e2)
    w2 = e2 / (1.0 + e2)
    lane8 = _lane_iota((tq, 8))
    route_ref[...] = jnp.where(lane8 == 0, i1.astype(F32),
                     jnp.where(lane8 == 1, i2.astype(F32),
                     jnp.where(lane8 == 2, w1, jnp.where(lane8 == 3, w2, 0.0))))


def _sb_layer(q, k, v, qc, km, vm, wo_bf16, g, h, g2, wr_split, batch, seq, tq=512):
    n = h.shape[0]
    mem_len = km.shape[0] // batch
    tiles = seq // tq
    row = lambda b, i: (b * tiles + i, 0)
    per_b = lambda b, i: (b, 0)
    fixed = lambda b, i: (0, 0)
    return pl.pallas_call(
        functools.partial(_sb_kernel, tq=tq),
        grid=(batch, tiles),
        in_specs=[
            pl.BlockSpec((tq, SELF_W), row),
            pl.BlockSpec((seq, SELF_W), per_b),
            pl.BlockSpec((seq, SELF_W), per_b),
            pl.BlockSpec((tq, CROSS_W), row),
            pl.BlockSpec((mem_len, CROSS_W), per_b),
            pl.BlockSpec((mem_len, CROSS_W), per_b),
            pl.BlockSpec((D_MODEL, D_MODEL), fixed),
            pl.BlockSpec((1, D_MODEL), fixed),
            pl.BlockSpec((tq, D_MODEL), row),
            pl.BlockSpec((1, D_MODEL), fixed),
            pl.BlockSpec((D_MODEL, 2 * LANES), fixed),
        ],
        out_specs=[
            pl.BlockSpec((tq, D_MODEL), row),
            pl.BlockSpec((tq, D_MODEL), row),
            pl.BlockSpec((tq, 8), row),
        ],
        out_shape=[
            jax.ShapeDtypeStruct((n, D_MODEL), F32),
            jax.ShapeDtypeStruct((n, D_MODEL), F32),
            jax.ShapeDtypeStruct((n, 8), F32),
        ],
        scratch_shapes=[
            pltpu.VMEM((tq, D_MODEL), BF16),
            pltpu.VMEM((2, SELF_W // LANES, tq, LANES), BF16),
            pltpu.VMEM((2, SELF_W // LANES, tq, LANES), F32),
            pltpu.VMEM((2, SELF_W // LANES, tq, LANES), F32),
        ],
        compiler_params=pltpu.CompilerParams(
            dimension_semantics=("arbitrary", "arbitrary"), vmem_limit_bytes=VMEM_LIMIT),
        name="sb_layer",
    )(q, k, v, qc, km, vm, wo_bf16, g, h, g2, wr_split)


def _swiglu_part(x_bf16, wg_ref, wu_ref, wd_ref):
    gate = _dot(x_bf16, wg_ref[...].astype(BF16))
    up = _dot(x_bf16, wu_ref[...].astype(BF16))
    act = (gate * jax.nn.sigmoid(gate) * up).astype(BF16)
    return _dot(act, wd_ref[...].astype(BF16))


def _dense_ffn_kernel(h_ref, g1_ref, wg_ref, wu_ref, wd_ref, g2_ref, o_ref, xn_ref):
    f = pl.program_id(1)

    @pl.when(f == 0)
    def _():
        xn_ref[...] = _rms(h_ref[...], g1_ref[...]).astype(BF16)
        o_ref[...] = jnp.zeros_like(o_ref)

    o_ref[...] += _swiglu_part(xn_ref[...], wg_ref, wu_ref, wd_ref)

    @pl.when(f == pl.num_programs(1) - 1)
    def _():
        o_ref[...] = h_ref[...] + _rms(o_ref[...], g2_ref[...])


def _dense_ffn(h, g1, wg, wu, wd, g2, tm=1024, tf=512):
    n = h.shape[0]
    row = lambda i, f: (i, 0)
    fixed = lambda i, f: (0, 0)
    return pl.pallas_call(
        _dense_ffn_kernel,
        grid=(n // tm, D_FF // tf),
        in_specs=[
            pl.BlockSpec((tm, D_MODEL), row),
            pl.BlockSpec((1, D_MODEL), fixed),
            pl.BlockSpec((D_MODEL, tf), lambda i, f: (0, f)),
            pl.BlockSpec((D_MODEL, tf), lambda i, f: (0, f)),
            pl.BlockSpec((tf, D_MODEL), lambda i, f: (f, 0)),
            pl.BlockSpec((1, D_MODEL), fixed),
        ],
        out_specs=pl.BlockSpec((tm, D_MODEL), row),
        out_shape=jax.ShapeDtypeStruct((n, D_MODEL), F32),
        scratch_shapes=[pltpu.VMEM((tm, D_MODEL), BF16)],
        compiler_params=pltpu.CompilerParams(
            dimension_semantics=("arbitrary", "arbitrary"), vmem_limit_bytes=VMEM_LIMIT),
        name="dense_ffn",
    )(h, g1, wg, wu, wd, g2)


def _moe_ffn_kernel(te_ref, used_ref, xs_ref, wg_ref, wu_ref, wd_ref, o_ref, xb_ref):
    i = pl.program_id(0)
    f = pl.program_id(1)

    @pl.when(i < used_ref[0])
    def _():
        @pl.when(f == 0)
        def _():
            xb_ref[...] = xs_ref[...].astype(BF16)
            o_ref[...] = jnp.zeros_like(o_ref)

        o_ref[...] += _swiglu_part(xb_ref[...], wg_ref, wu_ref, wd_ref)

    @pl.when((i >= used_ref[0]) & (f == pl.num_programs(1) - 1))
    def _():
        o_ref[...] = jnp.zeros_like(o_ref)


def _moe_ffn(tile_expert, used, xs, wg, wu, wd, tm, tf=512):
    p_rows = xs.shape[0]
    nf = D_FF // tf

    def f_eff(i, f, used_ref):
        return jnp.where(i < used_ref[0], f, nf - 1)

    grid_spec = pltpu.PrefetchScalarGridSpec(
        num_scalar_prefetch=2,
        grid=(p_rows // tm, nf),
        in_specs=[
            pl.BlockSpec((tm, D_MODEL), lambda i, f, te, u: (jnp.minimum(i, u[0] - 1), 0)),
            pl.BlockSpec((None, D_MODEL, tf), lambda i, f, te, u: (te[i], 0, f_eff(i, f, u))),
            pl.BlockSpec((None, D_MODEL, tf), lambda i, f, te, u: (te[i], 0, f_eff(i, f, u))),
            pl.BlockSpec((None, tf, D_MODEL), lambda i, f, te, u: (te[i], f_eff(i, f, u), 0)),
        ],
        out_specs=pl.BlockSpec((tm, D_MODEL), lambda i, f, te, u: (i, 0)),
        scratch_shapes=[pltpu.VMEM((tm, D_MODEL), BF16)],
    )
    return pl.pallas_call(
        _moe_ffn_kernel,
        grid_spec=grid_spec,
        out_shape=jax.ShapeDtypeStruct((p_rows, D_MODEL), F32),
        compiler_params=pltpu.CompilerParams(
            dimension_semantics=("arbitrary", "arbitrary"), vmem_limit_bytes=VMEM_LIMIT),
        name="moe_ffn",
    )(tile_expert, used, xs, wg, wu, wd)


def _dispatch_kernel(slot_ref, last_ref, xn_ref, xs_hbm, zero_ref, sems, *, tt, tm):
    i = pl.program_id(0)
    p_rows = xs_hbm.shape[0]

    def zero_tiles():
        for e in range(N_EXPERTS):
            yield last_ref[e], last_ref[e] >= 0
        for u in range(N_EXPERTS):
            start = last_ref[N_EXPERTS] + u * tm
            yield start, start < p_rows

    def zero_copy(start):
        return pltpu.make_async_copy(zero_ref, xs_hbm.at[pl.ds(pl.multiple_of(start, tm), tm)], sems.at[1])

    @pl.when(i == 0)
    def _():
        zero_ref[...] = jnp.zeros_like(zero_ref)
        for start, exists in zero_tiles():
            @pl.when(exists)
            def _():
                zero_copy(start).start()
        for start, exists in zero_tiles():
            @pl.when(exists)
            def _():
                zero_copy(start).wait()

    sem = sems.at[0]

    def issue(r, carry):
        t = i * tt + r
        src = xn_ref.at[pl.ds(r, 1)]
        pltpu.make_async_copy(src, xs_hbm.at[pl.ds(slot_ref[2 * t], 1)], sem).start(priority=0)
        pltpu.make_async_copy(src, xs_hbm.at[pl.ds(slot_ref[2 * t + 1], 1)], sem).start(priority=1)
        return carry

    lax.fori_loop(0, tt, issue, 0, unroll=True)

    for _ in range(2):
        pltpu.make_async_copy(xn_ref, xs_hbm.at[pl.ds(0, tt)], sem).wait()


def _dispatch(slot_flat, last_tile_start, xn, p_rows, tm, tt=2048):
    n = xn.shape[0]
    grid_spec = pltpu.PrefetchScalarGridSpec(
        num_scalar_prefetch=2,
        grid=(n // tt,),
        in_specs=[pl.BlockSpec((tt, D_MODEL), lambda i, s, l: (i, 0))],
        out_specs=pl.BlockSpec(memory_space=pl.ANY),
        scratch_shapes=[pltpu.VMEM((tm, D_MODEL), F32), pltpu.SemaphoreType.DMA((2,))],
    )
    return pl.pallas_call(
        functools.partial(_dispatch_kernel, tt=tt, tm=tm),
        grid_spec=grid_spec,
        out_shape=jax.ShapeDtypeStruct((p_rows, D_MODEL), F32),
        compiler_params=pltpu.CompilerParams(
            dimension_semantics=("arbitrary",), vmem_limit_bytes=VMEM_LIMIT),
        name="moe_dispatch",
    )(slot_flat, last_tile_start, xn)


def _combine_kernel(slot_ref, h_ref, route_ref, g_ref, ys_hbm, o_ref, ybuf, sems, *, tt):
    i = pl.program_id(0)
    steps = pl.num_programs(0)

    def issue_step(step, buf):
        def issue(r, carry):
            t = step * tt + r
            for k in range(2):
                pltpu.make_async_copy(ys_hbm.at[pl.ds(slot_ref[2 * t + k], 1)],
                                      ybuf.at[buf, k, pl.ds(r, 1)], sems.at[buf]).start()
            return carry
        lax.fori_loop(0, tt, issue, 0, unroll=True)

    @pl.when(i == 0)
    def _():
        issue_step(0, 0)

    @pl.when(i + 1 < steps)
    def _():
        issue_step(i + 1, (i + 1) % 2)

    buf = i % 2
    for k in range(2):
        pltpu.make_async_copy(ys_hbm.at[pl.ds(0, tt)], ybuf.at[buf, k], sems.at[buf]).wait()
    r = route_ref[...]
    y = r[:, 2:3] * ybuf[buf, 0] + r[:, 3:4] * ybuf[buf, 1]
    o_ref[...] = h_ref[...] + _rms(y, g_ref[...])


def _combine(slot_flat, h, route, g, ys, tt=256):
    n = h.shape[0]
    row = lambda i, s: (i, 0)
    grid_spec = pltpu.PrefetchScalarGridSpec(
        num_scalar_prefetch=1,
        grid=(n // tt,),
        in_specs=[
            pl.BlockSpec((tt, D_MODEL), row),
            pl.BlockSpec((tt, 8), row),
            pl.BlockSpec((1, D_MODEL), lambda i, s: (0, 0)),
            pl.BlockSpec(memory_space=pl.ANY),
        ],
        out_specs=pl.BlockSpec((tt, D_MODEL), row),
        scratch_shapes=[pltpu.VMEM((2, 2, tt, D_MODEL), F32), pltpu.SemaphoreType.DMA((2,))],
    )
    return pl.pallas_call(
        functools.partial(_combine_kernel, tt=tt),
        grid_spec=grid_spec,
        out_shape=jax.ShapeDtypeStruct((n, D_MODEL), F32),
        compiler_params=pltpu.CompilerParams(
            dimension_semantics=("arbitrary",), vmem_limit_bytes=VMEM_LIMIT),
        name="moe_combine",
    )(slot_flat, h, route, g, ys)


def _dispatch_plan(route, tm):
    n = route.shape[0]
    blk = 256
    ex = route[:, :2].astype(jnp.int32)
    eids = jnp.arange(N_EXPERTS, dtype=jnp.int32)[None, :]
    member = (ex[:, 0:1] == eids) | (ex[:, 1:2] == eids)
    mb = member.reshape(n // blk, blk, N_EXPERTS).astype(BF16)
    before = (jnp.arange(blk)[:, None] > jnp.arange(blk)[None, :]).astype(BF16)
    local = jnp.einsum('ij,bjk->bik', before, mb, preferred_element_type=F32)
    tot = jnp.sum(mb.astype(F32), axis=1)
    base = jnp.cumsum(tot, axis=0) - tot
    rank = (local + base[:, None, :]).reshape(n, N_EXPERTS).astype(jnp.int32)
    counts = jnp.sum(tot, axis=0).astype(jnp.int32)
    padded = ((counts + tm - 1) // tm) * tm
    ends = jnp.cumsum(padded)
    slot_all = (ends - padded)[None, :] + rank
    slot = jnp.stack([jnp.sum(jnp.where(ex[:, k:k + 1] == eids, slot_all, 0), axis=1) for k in range(2)], axis=1)
    p_rows = 2 * n + N_EXPERTS * tm
    last_tile_start = jnp.concatenate([jnp.where(padded > 0, ends - tm, -1), ends[-1:]]).astype(jnp.int32)
    tile_start = jnp.arange(p_rows // tm, dtype=jnp.int32) * tm
    tile_expert = jnp.sum((tile_start[:, None] >= ends[None, :]).astype(jnp.int32), axis=1)
    used = (ends[-1] // tm).astype(jnp.int32)
    last_expert = jnp.take(tile_expert, jnp.maximum(used - 1, 0))
    tile_expert = jnp.where(tile_start // tm < used, tile_expert, last_expert).astype(jnp.int32)
    return slot.reshape(-1), last_tile_start, tile_expert, used.reshape(1), p_rows


def kernel(x, mem, positions, l0_attn_pre_g, l0_mem_g, l0_w_in, l0_w_mem_kv, l0_sinks, l0_w_out, l0_attn_post_g, l0_ffn_pre_g, l0_w_gate, l0_w_up, l0_w_down, l0_ffn_post_g, l1_attn_pre_g, l1_mem_g, l1_w_in, l1_w_mem_kv, l1_w_out, l1_attn_post_g, l1_ffn_pre_g, l1_w_router, l1_w_gate, l1_w_up, l1_w_down, l1_ffn_post_g):
    batch, seq, _ = x.shape
    n = batch * seq
    h = x.reshape(n, D_MODEL)
    mem2d = mem.reshape(-1, D_MODEL)
    row = lambda g: g.reshape(1, D_MODEL)

    kv0 = SELF_W
    kw = N_KV_HEADS_SWA * HEAD_DIM
    dup = lambda w: jnp.tile(w.reshape(D_MODEL, N_KV_HEADS_SWA, 1, HEAD_DIM), (1, 1, 2, 1)).reshape(D_MODEL, KV_DUP_W)
    w_in0 = jnp.concatenate(
        [l0_w_in[:, :kv0], dup(l0_w_in[:, kv0:kv0 + kw]), dup(l0_w_in[:, kv0 + kw:kv0 + 2 * kw]),
         l0_w_in[:, kv0 + 2 * kw:]], axis=1).astype(BF16)
    inv_freq = ROPE_THETA ** (-jnp.arange(0, HEAD_DIM, 2, dtype=F32) / HEAD_DIM)
    invf = jnp.tile(inv_freq, LANES // (HEAD_DIM // 2)).reshape(1, LANES)

    km0, vm0 = _mem_kv(mem2d, row(l0_mem_g), l0_w_mem_kv.astype(BF16))
    q, k, v, qc = _proj_swa(h, row(l0_attn_pre_g), w_in0, positions.reshape(n, 1), invf)
    h = _swa_layer(l0_sinks, q, k, v, qc, km0, vm0, l0_w_out.astype(BF16), row(l0_attn_post_g), h, batch, seq)
    h = _dense_ffn(h, row(l0_ffn_pre_g), l0_w_gate, l0_w_up, l0_w_down, row(l0_ffn_post_g))

    km1, vm1 = _mem_kv(mem2d, row(l1_mem_g), l1_w_mem_kv.astype(BF16))
    q, k, v, qc = _proj_sb(h, row(l1_attn_pre_g), l1_w_in.astype(BF16))
    wr_pad = jnp.pad(l1_w_router, ((0, 0), (0, LANES - N_EXPERTS)))
    wr_hi = wr_pad.astype(BF16)
    wr_split = jnp.concatenate([wr_hi, (wr_pad - wr_hi.astype(F32)).astype(BF16)], axis=1)
    h, xn, route = _sb_layer(q, k, v, qc, km1, vm1, l1_w_out.astype(BF16), row(l1_attn_post_g), h,
                             row(l1_ffn_pre_g), wr_split, batch, seq)

    tm = 1024
    slot_flat, last_tile_start, tile_expert, used, p_rows = _dispatch_plan(route, tm)
    xs = _dispatch(slot_flat, last_tile_start, xn, p_rows, tm)
    ys = _moe_ffn(tile_expert, used, xs, l1_w_gate, l1_w_up, l1_w_down, tm)
    h = _combine(slot_flat, h, route, row(l1_ffn_post_g), ys)
    return h.reshape(batch, seq, D_MODEL)
```
